```python
import jax, jax.numpy as jnp
from jax import lax
import numpy as np


D_MODEL = 1024
BATCH = 8
SEQ = 4096
DEPTH = 2

NSA_HEADS = 8
NSA_GROUPS = 2
NSA_HPG = NSA_HEADS // NSA_GROUPS
NSA_DK = 64
CMP_LEN = 32
CMP_STRIDE = 16
CMP_HID = 256
SLC_BLOCK = 64
SLC_TOPN = 16
WINDOW = 512
Q_BLOCK = 128
SGU_CHUNK = 128
SGU_GROUPS = 4
SGU_WIDTH = 512
SGU_GROUP_DIM = SGU_WIDTH // SGU_GROUPS
RWKV_HEADS = 8
RWKV_HEAD_DIM = 64
RWKV_WIDTH = RWKV_HEADS * RWKV_HEAD_DIM
DECAY_LORA = 64
AAA_LORA = 64
MV_LORA = 32
GATE_LORA = 128
N_BRANCH = 3
BRANCH_WIDTH = 512
D_FF = 2816
MACARON_WEIGHT = 0.5
RMS_EPS = 1e-6
LN_EPS = 1e-5
LNX_EPS = 64e-5
NEG_INF = -1e30
FORCE_SCORE = 1e4

NSA_SPLITS = (NSA_HEADS * NSA_DK,) + (NSA_GROUPS * NSA_DK,) * 6 + (3 * NSA_HEADS,)
RWKV_SPLITS = (RWKV_WIDTH,) * 3 + (DECAY_LORA, AAA_LORA, GATE_LORA)
RWKV_COLS = sum(RWKV_SPLITS)
IN_SPLITS = (sum(NSA_SPLITS), 2 * SGU_WIDTH, RWKV_COLS, N_BRANCH * D_MODEL)
IN_COLS = sum(IN_SPLITS)

kernel_name = 'hybrid_nsa_sgu_rwkv7_macaron_block'


def _split(z, sizes):
    return jnp.split(z, np.cumsum(sizes)[:-1].tolist(), axis=-1)


def rms_norm(x, g):
    xf = x.astype(jnp.float32)
    y = xf * lax.rsqrt(jnp.mean(jnp.square(xf), -1, keepdims=True) + RMS_EPS)
    return (y * g).astype(x.dtype)


def layer_norm(x, g, b):
    xf = x.astype(jnp.float32)
    mu = xf.mean(-1, keepdims=True)
    var = jnp.mean(jnp.square(xf - mu), -1, keepdims=True)
    return ((xf - mu) * lax.rsqrt(var + LN_EPS) * g + b).astype(x.dtype)


def _pre(x, g, m):
    return rms_norm(x, g) * (1 + m[:, 1][:, None]) + m[:, 0][:, None]


def _swiglu(h, w_in, w_out):
    w_gate, w_up = w_in[:, :D_FF], w_in[:, D_FF:]
    return (jax.nn.silu(h @ w_gate) * (h @ w_up)) @ w_out


def _alibi_slopes():
    h = np.arange(1, NSA_HEADS + 1, dtype=np.float32)
    return jnp.asarray(2.0 ** (-8.0 * h / NSA_HEADS), jnp.float32).reshape(NSA_GROUPS, NSA_HPG)


def _masked_softmax(s, valid):
    s = jnp.where(valid, s, NEG_INF)
    return jnp.where(valid, jax.nn.softmax(s, axis=-1), 0.0)


def _selection_map(n_cmp, n_slc):
    rs, rc = SLC_BLOCK // CMP_STRIDE, CMP_LEN // CMP_STRIDE
    j = np.arange(n_slc)[:, None, None]
    i = np.broadcast_to(rs * j - np.arange(rs)[None, :, None] - np.arange(rc)[None, None, :], (n_slc, rs, rc))
    jj = np.broadcast_to(j, i.shape)
    ok = (i >= 0) & (i < n_cmp)
    m = np.zeros((n_cmp, n_slc), np.float32)
    np.add.at(m, (i[ok], jj[ok]), 1.0)
    return jnp.asarray(m)


def _compress(blk, pe, w1, w2):
    B, n, L, G, DK = blk.shape
    blk = (blk + pe[:, None, :]).transpose(0, 1, 3, 2, 4).reshape(B, n, G, L * DK)
    return jax.nn.silu(blk @ w1) @ w2


def nsa_mixer(p, k_w1, k_w2, k_pe, v_w1, v_w2, v_pe):
    B, S, _ = p.shape
    G, HPG, DK = NSA_GROUPS, NSA_HPG, NSA_DK
    f32 = jnp.float32
    q, kc, vc, ks, vs, kw, vw, gt = _split(p, NSA_SPLITS)
    q = (q * NSA_DK ** -0.5).reshape(B, S, G, HPG, DK)
    kc, vc, ks, vs, kw, vw = [z.reshape(B, S, G, DK) for z in (kc, vc, ks, vs, kw, vw)]
    gt = jax.nn.sigmoid(gt.astype(f32)).reshape(B, S, G, HPG, 3)
    n_cmp = (S - CMP_LEN) // CMP_STRIDE + 1
    cmp_idx = np.arange(n_cmp)[:, None] * CMP_STRIDE + np.arange(CMP_LEN)[None, :]
    kc = _compress(kc[:, cmp_idx], k_pe, k_w1, k_w2)
    vc = _compress(vc[:, cmp_idx], v_pe, v_w1, v_w2)
    cmp_end = jnp.asarray(cmp_idx[:, -1], jnp.int32)
    n_slc = S // SLC_BLOCK
    n_top = min(SLC_TOPN, n_slc)
    sel_map = _selection_map(n_cmp, n_slc)
    ks = ks.reshape(B, n_slc, SLC_BLOCK, G, DK).transpose(0, 3, 1, 2, 4)
    vs = vs.reshape(B, n_slc, SLC_BLOCK, G, DK).transpose(0, 3, 1, 2, 4)
    pad = ((0, 0), (WINDOW, 0), (0, 0), (0, 0))
    kw, vw = jnp.pad(kw, pad), jnp.pad(vw, pad)
    slopes = _alibi_slopes()
    b_ix = jnp.arange(B)[:, None, None, None]
    g_ix = jnp.arange(G)[None, :, None, None]
    slc_ids = jnp.arange(n_slc, dtype=jnp.int32)
    n_qb = S // Q_BLOCK

    def to_blocks(z):
        return jnp.moveaxis(z.reshape((B, n_qb, Q_BLOCK) + z.shape[2:]), 1, 0)

    def one_block(args):
        qb, gb, s0 = args
        t = s0 + jnp.arange(Q_BLOCK, dtype=jnp.int32)
        d_c = (t[:, None] - cmp_end[None, :]).astype(f32)
        s_c = jnp.einsum('bqghd,bngd->bghqn', qb, kc, preferred_element_type=f32)
        p_c = _masked_softmax(s_c - slopes[:, :, None, None] * d_c, d_c >= 0)
        o_c = jnp.einsum('bghqn,bngd->bqghd', p_c.astype(vc.dtype), vc, preferred_element_type=f32)
        imp = jnp.einsum('bghqn,nj->bgqj', p_c, sel_map)
        cur = (t // SLC_BLOCK)[:, None]
        forced = (slc_ids == 0) | (slc_ids == cur) | (slc_ids == cur - 1)
        live = slc_ids * SLC_BLOCK <= t[:, None]
        imp = jnp.where(forced, FORCE_SCORE, jnp.where(live, imp, NEG_INF))
        _, sel = lax.top_k(imp, n_top)
        k_sel = ks[b_ix, g_ix, sel].reshape(B, G, Q_BLOCK, n_top * SLC_BLOCK, DK)
        v_sel = vs[b_ix, g_ix, sel].reshape(B, G, Q_BLOCK, n_top * SLC_BLOCK, DK)
        pos_s = (sel[..., None] * SLC_BLOCK + jnp.arange(SLC_BLOCK, dtype=jnp.int32)).reshape(B, G, Q_BLOCK, n_top * SLC_BLOCK)
        d_s = (t[:, None] - pos_s).astype(f32)[:, :, None]
        s_s = jnp.einsum('bqghd,bgqkd->bghqk', qb, k_sel, preferred_element_type=f32)
        p_s = _masked_softmax(s_s - slopes[:, :, None, None] * d_s, d_s >= 0)
        o_s = jnp.einsum('bghqk,bgqkd->bqghd', p_s.astype(v_sel.dtype), v_sel, preferred_element_type=f32)
        k_win = lax.dynamic_slice_in_dim(kw, s0, WINDOW + Q_BLOCK, axis=1)
        v_win = lax.dynamic_slice_in_dim(vw, s0, WINDOW + Q_BLOCK, axis=1)
        pos_w = s0 - WINDOW + jnp.arange(WINDOW + Q_BLOCK, dtype=jnp.int32)
        d_w = t[:, None] - pos_w[None, :]
        valid_w = (d_w >= 0) & (d_w < WINDOW) & (pos_w[None, :] >= 0)
        s_w = jnp.einsum('bqghd,bkgd->bghqk', qb, k_win, preferred_element_type=f32)
        p_w = _masked_softmax(s_w - slopes[:, :, None, None] * d_w.astype(f32), valid_w)
        o_w = jnp.einsum('bghqk,bkgd->bqghd', p_w.astype(v_win.dtype), v_win, preferred_element_type=f32)
        o = gb[..., 0:1] * o_c + gb[..., 1:2] * o_s + gb[..., 2:3] * o_w
        return o.reshape(B, Q_BLOCK, NSA_HEADS * NSA_DK).astype(p.dtype)

    starts = jnp.arange(n_qb, dtype=jnp.int32) * Q_BLOCK
    out = lax.map(one_block, (to_blocks(q), to_blocks(gt), starts))
    return jnp.moveaxis(out, 0, 1).reshape(B, S, NSA_HEADS * NSA_DK)


def sgu_mixer(p, ln_g, ln_b, w_s, b_s):
    B, S, _ = p.shape
    u, v = jnp.split(jax.nn.gelu(p), 2, axis=-1)
    v = layer_norm(v, ln_g, ln_b)
    v = v.reshape(B, S // SGU_CHUNK, SGU_CHUNK, SGU_GROUPS, SGU_GROUP_DIM)
    causal = jnp.tril(jnp.ones((SGU_CHUNK, SGU_CHUNK), dtype=bool))
    w = jnp.where(causal, w_s, 0).astype(v.dtype)
    s = jnp.einsum('gts,bnsgc->bntgc', w, v) + b_s.T[:, :, None]
    return u * s.reshape(B, S, SGU_WIDTH)


def rwkv7_mixer(p, mu, w0, w2, a0, a2, g2, k_k, k_a, r_k, lnx_g, lnx_b, v_first, v_res):
    B, S, _ = p.shape
    H, N = RWKV_HEADS, RWKV_HEAD_DIM
    f32 = jnp.float32
    p_prev = jnp.pad(p, ((0, 0), (1, 0), (0, 0)))[:, :-1]
    p = p + (p_prev - p) * mu
    r, k, v, wd, ad, gd = _split(p, RWKV_SPLITS)
    w = -jax.nn.softplus(-(w0 + jnp.tanh(wd) @ w2)) - 0.5
    a = jax.nn.sigmoid(a0 + ad @ a2)
    g = jax.nn.sigmoid(gd) @ g2
    if v_res is None:
        v_first = v
    else:
        v0, v1, v2 = v_res
        v = v + (v_first - v) * jax.nn.sigmoid(v0 + (v @ v1) @ v2)

    def heads(z):
        return z.astype(f32).reshape(B, S, H, N)

    kk = heads(k * k_k)
    kk = kk / jnp.maximum(jnp.linalg.norm(kk, axis=-1, keepdims=True), 1e-12)
    k = k * (1 + (a - 1) * k_a)
    rh, kh, vh, ah = heads(r), heads(k), heads(v), heads(a)
    decay = jnp.exp(-jnp.exp(heads(w)))
    xs = tuple(jnp.moveaxis(z, 1, 0) for z in (rh, decay, kh, vh, -kk, kk * ah))

    def step(state, inp):
        r_t, w_t, k_t, v_t, a_t, b_t = inp
        sa = jnp.einsum('bhij,bhj->bhi', state, a_t)
        state = state * w_t[:, :, None, :] + sa[..., None] * b_t[:, :, None, :] + v_t[..., None] * k_t[:, :, None, :]
        return state, jnp.einsum('bhij,bhj->bhi', state, r_t)

    _, y = lax.scan(step, jnp.zeros((B, H, N, N), f32), xs)
    y = jnp.moveaxis(y, 0, 1)
    mean = y.mean(-1, keepdims=True)
    var = jnp.mean(jnp.square(y - mean), -1, keepdims=True)
    y = ((y - mean) * lax.rsqrt(var + LNX_EPS)).reshape(B, S, RWKV_WIDTH) * lnx_g + lnx_b
    bonus = (jnp.sum(rh * kh * r_k, -1, keepdims=True) * vh).reshape(B, S, RWKV_WIDTH)
    y = (y + bonus) * g
    return y.astype(p.dtype), v_first


def setup_inputs(seed: int = 0) -> dict:
    key = jax.random.key(seed)
    ks = iter(jax.random.split(key, 40))
    L, D = DEPTH, D_MODEL

    def nrm(shape, s):
        return s * jax.random.normal(next(ks), shape, jnp.float32)

    return {
        'x': nrm((BATCH, SEQ, D), 1.0),
        'c': nrm((BATCH, D), 1.0),
        'ada_w': nrm((L, D, 9 * D), 0.5 * D ** -0.5),
        'ada_b': nrm((L, 9 * D), 0.02),
        'pre_g': 1.0 + nrm((L, 3, D), 0.1),
        'post_g': 1.0 + nrm((L, 3, D), 0.1),
        'ffn_w_in': nrm((L, 2, D, 2 * D_FF), D ** -0.5),
        'ffn_w_out': nrm((L, 2, D_FF, D), D_FF ** -0.5),
        'mix_w_in': nrm((L, D, IN_COLS), D ** -0.5),
        'branch_w': nrm((L, N_BRANCH, BRANCH_WIDTH, D), BRANCH_WIDTH ** -0.5),
        'out_w': nrm((L, D, D), D ** -0.5),
        'cmp_k_w1': nrm((L, CMP_LEN * NSA_DK, CMP_HID), (CMP_LEN * NSA_DK) ** -0.5),
        'cmp_k_w2': nrm((L, CMP_HID, NSA_DK), CMP_HID ** -0.5),
        'cmp_k_pe': nrm((L, CMP_LEN, NSA_DK), 0.1),
        'cmp_v_w1': nrm((L, CMP_LEN * NSA_DK, CMP_HID), (CMP_LEN * NSA_DK) ** -0.5),
        'cmp_v_w2': nrm((L, CMP_HID, NSA_DK), CMP_HID ** -0.5),
        'cmp_v_pe': nrm((L, CMP_LEN, NSA_DK), 0.1),
        'sgu_ln_g': 1.0 + nrm((L, SGU_WIDTH), 0.1),
        'sgu_ln_b': nrm((L, SGU_WIDTH), 0.02),
        'sgu_w': nrm((L, SGU_GROUPS, SGU_CHUNK, SGU_CHUNK), 0.5 * SGU_CHUNK ** -0.5),
        'sgu_b': 1.0 + nrm((L, SGU_GROUPS, SGU_CHUNK), 0.1),
        'rwkv_mu': jax.random.uniform(next(ks), (L, RWKV_COLS), jnp.float32),
        'rwkv_w0': -1.0 + nrm((L, RWKV_WIDTH), 0.3),
        'rwkv_w2': nrm((L, DECAY_LORA, RWKV_WIDTH), 0.5 * DECAY_LORA ** -0.5),
        'rwkv_a0': nrm((L, RWKV_WIDTH), 0.1),
        'rwkv_a2': nrm((L, AAA_LORA, RWKV_WIDTH), 0.5 * AAA_LORA ** -0.5),
        'rwkv_g2': nrm((L, GATE_LORA, RWKV_WIDTH), GATE_LORA ** -0.5),
        'rwkv_kk': 1.0 + nrm((L, RWKV_WIDTH), 0.1),
        'rwkv_ka': 1.0 + nrm((L, RWKV_WIDTH), 0.1),
        'rwkv_rk': nrm((L, RWKV_HEADS, RWKV_HEAD_DIM), 0.1),
        'rwkv_lnx_g': 1.0 + nrm((L, RWKV_WIDTH), 0.1),
        'rwkv_lnx_b': nrm((L, RWKV_WIDTH), 0.02),
        'rwkv_v0': nrm((L - 1, RWKV_WIDTH), 0.1),
        'rwkv_v1': nrm((L - 1, RWKV_WIDTH, MV_LORA), RWKV_WIDTH ** -0.5),
        'rwkv_v2': nrm((L - 1, MV_LORA, RWKV_WIDTH), MV_LORA ** -0.5),
    }


def reference(x, c, ada_w, ada_b, pre_g, post_g, ffn_w_in, ffn_w_out, mix_w_in, branch_w, out_w,
              cmp_k_w1, cmp_k_w2, cmp_k_pe, cmp_v_w1, cmp_v_w2, cmp_v_pe,
              sgu_ln_g, sgu_ln_b, sgu_w, sgu_b,
              rwkv_mu, rwkv_w0, rwkv_w2, rwkv_a0, rwkv_a2, rwkv_g2, rwkv_kk, rwkv_ka, rwkv_rk,
              rwkv_lnx_g, rwkv_lnx_b, rwkv_v0, rwkv_v1, rwkv_v2):
    B = x.shape[0]
    cond = jax.nn.silu(c)
    v_first = None
    for l in range(DEPTH):
        mod = (cond @ ada_w[l] + ada_b[l]).reshape(B, 3, 3, D_MODEL)
        h = _pre(x, pre_g[l, 0], mod[:, 0])
        y = _swiglu(h, ffn_w_in[l, 0], ffn_w_out[l, 0])
        x = x + MACARON_WEIGHT * mod[:, 0, 2][:, None] * rms_norm(y, post_g[l, 0])
        h = _pre(x, pre_g[l, 1], mod[:, 1])
        w_nsa, w_sgu, w_rwkv, w_gate = _split(mix_w_in[l], IN_SPLITS)
        y_a = nsa_mixer(h @ w_nsa, cmp_k_w1[l], cmp_k_w2[l], cmp_k_pe[l], cmp_v_w1[l], cmp_v_w2[l], cmp_v_pe[l])
        y_b = sgu_mixer(h @ w_sgu, sgu_ln_g[l], sgu_ln_b[l], sgu_w[l], sgu_b[l])
        v_res = None if l == 0 else (rwkv_v0[l - 1], rwkv_v1[l - 1], rwkv_v2[l - 1])
        y_c, v_first = rwkv7_mixer(h @ w_rwkv, rwkv_mu[l], rwkv_w0[l], rwkv_w2[l], rwkv_a0[l], rwkv_a2[l],
                                   rwkv_g2[l], rwkv_kk[l], rwkv_ka[l], rwkv_rk[l], rwkv_lnx_g[l], rwkv_lnx_b[l],
                                   v_first, v_res)
        gate_cols = _split(w_gate, (D_MODEL,) * N_BRANCH)
        merged = 0.0
        for i, (w_g, y_i) in enumerate(zip(gate_cols, (y_a, y_b, y_c))):
            merged = merged + jax.nn.sigmoid(h @ w_g) * (y_i @ branch_w[l, i])
        y = merged @ out_w[l]
        x = x + mod[:, 1, 2][:, None] * rms_norm(y, post_g[l, 1])
        h = _pre(x, pre_g[l, 2], mod[:, 2])
        y = _swiglu(h, ffn_w_in[l, 1], ffn_w_out[l, 1])
        x = x + MACARON_WEIGHT * mod[:, 2, 2][:, None] * rms_norm(y, post_g[l, 2])
    return x
```

```python
import functools

import numpy as np
import jax
import jax.numpy as jnp
from jax import lax
from jax.experimental import pallas as pl
from jax.experimental.pallas import tpu as pltpu

D_MODEL = 1024
DEPTH = 2
NSA_HEADS = 8
NSA_GROUPS = 2
NSA_HPG = NSA_HEADS // NSA_GROUPS
NSA_DK = 64
CMP_LEN = 32
CMP_STRIDE = 16
CMP_HID = 256
SLC_BLOCK = 64
SLC_TOPN = 16
WINDOW = 512
Q_BLOCK = 128
SGU_CHUNK = 128
SGU_GROUPS = 4
SGU_WIDTH = 512
RWKV_HEADS = 8
RWKV_HEAD_DIM = 64
RWKV_WIDTH = RWKV_HEADS * RWKV_HEAD_DIM
DECAY_LORA = 64
AAA_LORA = 64
MV_LORA = 32
GATE_LORA = 128
N_BRANCH = 3
BRANCH_WIDTH = 512
D_FF = 2816
MACARON_WEIGHT = 0.5
RMS_EPS = 1e-6
LN_EPS = 1e-5
LNX_EPS = 64e-5
NEG_INF = -1e30
FORCE_SCORE = 1e4

NSA_COLS = NSA_HEADS * NSA_DK + 6 * NSA_GROUPS * NSA_DK + 3 * NSA_HEADS
RWKV_COLS = 3 * RWKV_WIDTH + DECAY_LORA + AAA_LORA + GATE_LORA
Q_COLS = NSA_HEADS * NSA_DK
KV_COLS = 6 * NSA_GROUPS * NSA_DK
GT_COLS = 3 * NSA_HEADS
LANES = 128

BF16 = jnp.bfloat16
F32 = jnp.float32

FFN_TILE = 512
FFN_CHUNK = 1408
SEL_TILE = 512
RWKV_CHUNK = 64
RWKV_PREP_TILE = 256
SGU_TILE = 512
VMEM_LIMIT = 56 * 1024 * 1024


def _cparams(*sem):
    return pltpu.CompilerParams(dimension_semantics=sem, vmem_limit_bytes=VMEM_LIMIT)


def _dot(a, b):
    return jnp.dot(a.astype(BF16), b.astype(BF16), preferred_element_type=F32)


def _dot_nt(a, b):
    return lax.dot_general(a.astype(BF16), b.astype(BF16), (((1,), (1,)), ((), ())),
                           preferred_element_type=F32)


def _dot_tn(a, b):
    return lax.dot_general(a.astype(BF16), b.astype(BF16), (((0,), (0,)), ((), ())),
                           preferred_element_type=F32)


def _split(x):
    hi = x.astype(BF16)
    lo = (x - hi.astype(F32)).astype(BF16)
    return hi, lo


def _dot_x2(a, b):
    hi, lo = _split(a)
    return _dot(hi, b) + _dot(lo, b)


def _dot3(a, b, fn=_dot):
    ah, al = _split(a)
    bh, bl = _split(b)
    return fn(ah, bh) + (fn(ah, bl) + fn(al, bh))


def _sigmoid(x):
    return 1.0 / (1.0 + jnp.exp(-x))


def _silu(x):
    return x * _sigmoid(x)


def _rms(x, g):
    return x * lax.rsqrt(jnp.mean(x * x, axis=-1, keepdims=True) + RMS_EPS) * g


def _pre(x, g, mod_ref):
    return _rms(x, g) * (1.0 + mod_ref[1:2, :]) + mod_ref[0:1, :]


def _ada_kernel(c_ref, w_ref, b_ref, o_ref):
    cond = _silu(c_ref[...])
    o_ref[...] = _dot(cond, w_ref[...]) + b_ref[...]


def _ada_mod(c, ada_w, ada_b):
    L, D, N = ada_w.shape
    B = c.shape[0]
    tn = 1536
    return pl.pallas_call(
        _ada_kernel,
        grid=(L, N // tn),
        in_specs=[
            pl.BlockSpec((B, D), lambda l, j: (0, 0)),
            pl.BlockSpec((None, D, tn), lambda l, j: (l, 0, j)),
            pl.BlockSpec((None, 1, tn), lambda l, j: (l, 0, j)),
        ],
        out_specs=pl.BlockSpec((None, B, tn), lambda l, j: (l, 0, j)),
        out_shape=jax.ShapeDtypeStruct((L, B, N), F32),
        compiler_params=_cparams("arbitrary", "arbitrary"),
        name="ada_mod",
    )(c, ada_w, ada_b.reshape(L, 1, N))


def _ffn_kernel(x_ref, mod_ref, pg_ref, qg_ref, win_ref, wout_ref, o_ref):
    x = x_ref[...]
    h = _pre(x, pg_ref[...], mod_ref).astype(BF16)
    acc = None
    for c in range(D_FF // FFN_CHUNK):
        lo, hi = c * FFN_CHUNK, (c + 1) * FFN_CHUNK
        gate = jnp.dot(h, win_ref[:, lo:hi], preferred_element_type=F32)
        up = jnp.dot(h, win_ref[:, D_FF + lo:D_FF + hi], preferred_element_type=F32)
        act = (_silu(gate) * up).astype(BF16)
        part = jnp.dot(act, wout_ref[lo:hi, :], preferred_element_type=F32)
        acc = part if acc is None else acc + part
    o_ref[...] = x + MACARON_WEIGHT * mod_ref[2:3, :] * _rms(acc, qg_ref[...])


def _const_spec(shape):
    nd = len(shape)
    return pl.BlockSpec(shape, lambda *_: (0,) * nd, pipeline_mode=pl.Buffered(1))


def _ffn(x, mod3, pre_g, post_g, w_in, w_out):
    B, S, D = x.shape
    tm = min(FFN_TILE, S)
    return pl.pallas_call(
        _ffn_kernel,
        grid=(B, S // tm),
        in_specs=[
            pl.BlockSpec((None, tm, D), lambda b, i: (b, i, 0)),
            pl.BlockSpec((None, 3, D), lambda b, i: (b, 0, 0)),
            _const_spec((1, D)),
            _const_spec((1, D)),
            _const_spec((D, 2 * D_FF)),
            _const_spec((D_FF, D)),
        ],
        out_specs=pl.BlockSpec((None, tm, D), lambda b, i: (b, i, 0)),
        out_shape=jax.ShapeDtypeStruct((B, S, D), F32),
        compiler_params=_cparams("parallel", "parallel"),
        name="ffn",
    )(x, mod3, pre_g.reshape(1, D), post_g.reshape(1, D), w_in, w_out)


PROJ_COLS = Q_COLS + KV_COLS + LANES + 2 * SGU_WIDTH + RWKV_COLS


def _proj_kernel(x_ref, mod_ref, pg_ref, w_ref, q_ref, kv_ref, gt_ref, sgu_ref, rw_ref):
    h = _pre(x_ref[...], pg_ref[...], mod_ref).astype(BF16)
    c0, c1, c2, c3 = Q_COLS, Q_COLS + KV_COLS, Q_COLS + KV_COLS + LANES, PROJ_COLS - RWKV_COLS
    q = jnp.dot(h, w_ref[:, 0:c0], preferred_element_type=F32)
    q_ref[...] = (q * NSA_DK ** -0.5).astype(BF16)
    kv_ref[...] = jnp.dot(h, w_ref[:, c0:c1], preferred_element_type=F32).astype(BF16)
    gt_ref[...] = jnp.dot(h, w_ref[:, c1:c2], preferred_element_type=F32)
    sgu_ref[...] = jnp.dot(h, w_ref[:, c2:c3], preferred_element_type=F32)
    rw_ref[...] = jnp.dot(h, w_ref[:, c3:PROJ_COLS], preferred_element_type=F32)


def _mix_proj(x, mod3, pre_g, w):
    B, S, D = x.shape
    tm = min(FFN_TILE, S)
    widths = (Q_COLS, KV_COLS, LANES, 2 * SGU_WIDTH, RWKV_COLS)
    dtypes = (BF16, BF16, F32, F32, F32)
    return pl.pallas_call(
        _proj_kernel,
        grid=(B, S // tm),
        in_specs=[
            pl.BlockSpec((None, tm, D), lambda b, i: (b, i, 0)),
            pl.BlockSpec((None, 3, D), lambda b, i: (b, 0, 0)),
            _const_spec((1, D)),
            _const_spec((D, PROJ_COLS)),
        ],
        out_specs=[pl.BlockSpec((None, tm, n), lambda b, i: (b, i, 0)) for n in widths],
        out_shape=[jax.ShapeDtypeStruct((B, S, n), dt) for n, dt in zip(widths, dtypes)],
        compiler_params=_cparams("parallel", "parallel"),
        name="mix_proj",
    )(x, mod3, pre_g.reshape(1, D), w)


def _cmp_kernel(xk_ref, xv_ref, pek_ref, pev_ref, w1k_ref, w1v_ref, w2kt_ref, w2v_ref,
                kct_ref, vc_ref):
    half = (CMP_LEN // 2) * NSA_DK

    def hidden(x_ref, pe_ref, w1_ref):
        x = x_ref[...].astype(F32)
        n = x.shape[0]
        top = _dot(x + pe_ref[0:1, :], w1_ref[0:half, :])
        bot = _dot(x + pe_ref[1:2, :], w1_ref[half:2 * half, :])
        return _silu(top + pltpu.roll(bot, n - 1, axis=0))

    kct_ref[...] = _dot_nt(w2kt_ref[...], hidden(xk_ref, pek_ref, w1k_ref)).astype(BF16)
    vc_ref[...] = _dot(hidden(xv_ref, pev_ref, w1v_ref), w2v_ref[...]).astype(BF16)


def _compress(xk, xv, pe_k, pe_v, w1k, w1v, w2kt, w2v):
    B, G, NC, F = xk.shape
    blk = lambda *s: pl.BlockSpec((None, None) + s, lambda b, g: (b, g, 0, 0))
    return pl.pallas_call(
        _cmp_kernel,
        grid=(B, G),
        in_specs=[blk(NC, F), blk(NC, F), _const_spec((2, F)), _const_spec((2, F)),
                  _const_spec((2 * F, CMP_HID)), _const_spec((2 * F, CMP_HID)),
                  _const_spec((NSA_DK, CMP_HID)), _const_spec((CMP_HID, NSA_DK))],
        out_specs=[blk(NSA_DK, NC), blk(NC, NSA_DK)],
        out_shape=[jax.ShapeDtypeStruct((B, G, NSA_DK, NC), BF16),
                   jax.ShapeDtypeStruct((B, G, NC, NSA_DK), BF16)],
        compiler_params=_cparams("parallel", "parallel"),
        name="nsa_compress",
    )(xk, xv, pe_k, pe_v, w1k, w1v, w2kt, w2v)


def _masked_softmax(s, valid):
    s = jnp.where(valid, s, NEG_INF)
    m = jnp.max(s, axis=-1, keepdims=True)
    p = jnp.where(valid, jnp.exp(s - m), 0.0)
    l = jnp.sum(p, axis=-1, keepdims=True)
    return p * (1.0 / jnp.where(l > 0.0, l, 1.0))


def _nsa_kernel(q_ref, gt_ref, kct_ref, vc_ref, kst_ref, vs_ref, kwt_ref, vw_ref, selt_ref,
                o_ref, imp_ref, *, n_top, tk):
    g = pl.program_id(1)
    qb = pl.program_id(2)
    s0 = qb * Q_BLOCK
    R = NSA_HPG * Q_BLOCK
    n_slc = selt_ref.shape[0]
    n_cmp = kct_ref.shape[-1]
    q = q_ref[...].reshape(R, NSA_DK)

    row = lax.broadcasted_iota(jnp.int32, (R, 1), 0)
    t_f = (s0 + (row & (Q_BLOCK - 1))).astype(F32)
    head = g * NSA_HPG + row // Q_BLOCK + 1
    slope = lax.bitcast_convert_type((127 - head) << 23, F32)

    s_c = jnp.dot(q, kct_ref[...], preferred_element_type=F32)
    cmp_end = (lax.broadcasted_iota(jnp.int32, (1, n_cmp), 1) * CMP_STRIDE + (CMP_LEN - 1)).astype(F32)
    d_c = t_f - cmp_end
    p_c = _masked_softmax(s_c - slope * d_c, d_c >= 0.0)
    o_c = _dot(p_c, vc_ref[...])

    p_sum = p_c[0:Q_BLOCK]
    for h in range(1, NSA_HPG):
        p_sum = p_sum + p_c[h * Q_BLOCK:(h + 1) * Q_BLOCK]
    hi, lo = _split(p_sum)
    imp = _dot_nt(selt_ref[...], hi) + _dot_nt(selt_ref[...], lo)
    j = lax.broadcasted_iota(jnp.int32, (n_slc, Q_BLOCK), 0)
    tq = s0 + lax.broadcasted_iota(jnp.int32, (n_slc, Q_BLOCK), 1)
    cur = tq // SLC_BLOCK
    forced = (j == 0) | (j == cur) | (j == cur - 1)
    live = j * SLC_BLOCK <= tq
    imp = jnp.where(forced, FORCE_SCORE, jnp.where(live, imp, NEG_INF))
    imp_ref[...] = imp

    def rank_body(i, cnt):
        r = imp_ref[pl.ds(i, 1), :]
        tie = jnp.where(j > i, 1.0, 0.0)
        return cnt + jnp.where(r > imp, 1.0, jnp.where(r == imp, tie, 0.0))

    cnt = lax.fori_loop(0, n_slc, rank_body, jnp.zeros((n_slc, Q_BLOCK), F32))
    member = jnp.where(cnt < n_top, 1.0, 0.0).astype(BF16)
    member = jnp.concatenate([member] * NSA_HPG, axis=1)

    def sel_body(kt, carry):
        m, l, acc = carry
        s = jnp.dot(q, kst_ref[kt], preferred_element_type=F32)
        blk = lax.broadcasted_iota(jnp.int32, (n_slc, tk), 1) // SLC_BLOCK + kt * (tk // SLC_BLOCK)
        expand = jnp.where(lax.broadcasted_iota(jnp.int32, (n_slc, tk), 0) == blk, 1.0, 0.0)
        chosen = _dot_tn(member, expand)
        pos = (kt * tk + lax.broadcasted_iota(jnp.int32, (1, tk), 1)).astype(F32)
        d = t_f - pos
        valid = (chosen > 0.5) & (d >= 0.0)
        s = jnp.where(valid, s - slope * d, NEG_INF)
        m_new = jnp.maximum(m, jnp.max(s, axis=-1, keepdims=True))
        alpha = jnp.exp(m - m_new)
        p = jnp.where(valid, jnp.exp(s - m_new), 0.0)
        l = alpha * l + jnp.sum(p, axis=-1, keepdims=True)
        v = vs_ref[pl.ds(pl.multiple_of(kt * tk, tk), tk), :]
        acc = alpha * acc + _dot(p, v)
        return m_new, l, acc

    n_kt = (s0 + Q_BLOCK + tk - 1) // tk
    init = (jnp.full((R, 1), NEG_INF, F32), jnp.zeros((R, 1), F32), jnp.zeros((R, NSA_DK), F32))
    _, l_s, acc_s = lax.fori_loop(0, n_kt, sel_body, init)
    o_s = acc_s * (1.0 / jnp.where(l_s > 0.0, l_s, 1.0))

    start = jnp.maximum(s0 - WINDOW, 0)
    wt0 = start // LANES
    n_wt = (WINDOW + Q_BLOCK) // LANES
    s_w = jnp.concatenate(
        [jnp.dot(q, kwt_ref[wt0 + i], preferred_element_type=F32) for i in range(n_wt)], axis=1)
    pos_w = (start + lax.broadcasted_iota(jnp.int32, (1, n_wt * LANES), 1)).astype(F32)
    d_w = t_f - pos_w
    p_w = _masked_softmax(s_w - slope * d_w, (d_w >= 0.0) & (d_w < float(WINDOW)))
    o_w = _dot(p_w, vw_ref[pl.ds(pl.multiple_of(start, LANES), n_wt * LANES), :])

    gate = _sigmoid(gt_ref[...].reshape(R, 3))
    o = gate[:, 0:1] * o_c + gate[:, 1:2] * o_s + gate[:, 2:3] * o_w
    o_ref[...] = o.reshape(NSA_HPG, Q_BLOCK, NSA_DK).astype(o_ref.dtype)


def _selection_map_t(n_cmp_rows, n_slc):
    rs, rc = SLC_BLOCK // CMP_STRIDE, CMP_LEN // CMP_STRIDE
    m = np.zeros((n_slc, n_cmp_rows), np.float32)
    for jj in range(n_slc):
        for a in range(rs):
            for b in range(rc):
                i = rs * jj - a - b
                if 0 <= i < n_cmp_rows - 1:
                    m[jj, i] += 1.0
    return jnp.asarray(m, BF16)


def _nsa_attention(q, gt, kct, vc, kst, vs, kwt, vw):
    B, H, S, DK = q.shape
    G = NSA_GROUPS
    n_slc = S // SLC_BLOCK
    n_cmp = kct.shape[-1]
    tk = kst.shape[-1]
    selt = _selection_map_t(n_cmp, n_slc)
    per_q = lambda d: pl.BlockSpec((None, NSA_HPG, Q_BLOCK, d), lambda b, g, i: (b, g, i, 0))
    per_g = lambda *s: pl.BlockSpec((None, None) + s, lambda b, g, i: (b, g) + (0,) * len(s))
    kern = functools.partial(_nsa_kernel, n_top=min(SLC_TOPN, n_slc), tk=tk)
    return pl.pallas_call(
        kern,
        grid=(B, G, S // Q_BLOCK),
        in_specs=[per_q(DK), per_q(3), per_g(DK, n_cmp), per_g(n_cmp, DK),
                  per_g(S // tk, DK, tk), per_g(S, DK), per_g(S // LANES, DK, LANES), per_g(S, DK),
                  pl.BlockSpec((n_slc, n_cmp), lambda b, g, i: (0, 0))],
        out_specs=per_q(DK),
        out_shape=jax.ShapeDtypeStruct((B, H, S, DK), BF16),
        scratch_shapes=[pltpu.VMEM((n_slc, Q_BLOCK), F32)],
        compiler_params=_cparams("parallel", "parallel", "arbitrary"),
        name="nsa_attention",
    )(q, gt, kct, vc, kst, vs, kwt, vw, selt)


def _nsa_mixer(q, kv, gt, k_w1, k_w2, k_pe, v_w1, v_w2, v_pe):
    B, S, _ = q.shape
    G, DK = NSA_GROUPS, NSA_DK
    kc, vc, ks, vs, kw, vw = [kv[:, :, i * G * DK:(i + 1) * G * DK].reshape(B, S, G, DK) for i in range(6)]
    half = CMP_LEN // 2
    nc = S // half

    def half_blocks(z):
        return z.reshape(B, nc, half, G, DK).transpose(0, 3, 1, 2, 4).reshape(B, G, nc, half * DK)

    kct, vcc = _compress(
        half_blocks(kc), half_blocks(vc), k_pe.reshape(2, half * DK), v_pe.reshape(2, half * DK),
        k_w1.astype(BF16), v_w1.astype(BF16), k_w2.T.astype(BF16), v_w2.astype(BF16))

    def key_tiles(z, t):
        return z.reshape(B, S // t, t, G, DK).transpose(0, 3, 1, 4, 2)

    tk = min(SEL_TILE, S)
    qh = q.reshape(B, S, NSA_HEADS, DK).transpose(0, 2, 1, 3)
    gth = gt[:, :, :GT_COLS].reshape(B, S, NSA_HEADS, 3).transpose(0, 2, 1, 3)
    o = _nsa_attention(qh, gth, kct, vcc, key_tiles(ks, tk), vs.transpose(0, 2, 1, 3),
                       key_tiles(kw, LANES), vw.transpose(0, 2, 1, 3))
    return o.transpose(0, 2, 1, 3).reshape(B, S, NSA_HEADS * DK)


def _gelu_tanh(x):
    return 0.5 * x * (1.0 + jnp.tanh(np.sqrt(2.0 / np.pi).astype(np.float32) * (x + 0.044715 * (x * x * x))))


def _sgu_kernel(p_ref, g_ref, b_ref, w_ref, bs_ref, o_ref):
    ts = p_ref.shape[0]
    gd = SGU_WIDTH // SGU_GROUPS
    u = _gelu_tanh(p_ref[:, 0:SGU_WIDTH])
    v = _gelu_tanh(p_ref[:, SGU_WIDTH:2 * SGU_WIDTH])
    mu = jnp.mean(v, axis=-1, keepdims=True)
    vc = v - mu
    var = jnp.mean(vc * vc, axis=-1, keepdims=True)
    vn = (vc * lax.rsqrt(var + LN_EPS) * g_ref[...] + b_ref[...]).astype(BF16)
    causal = (lax.broadcasted_iota(jnp.int32, (SGU_CHUNK, SGU_CHUNK), 0)
              >= lax.broadcasted_iota(jnp.int32, (SGU_CHUNK, SGU_CHUNK), 1))
    ws = [jnp.where(causal, w_ref[i], 0.0).astype(BF16) for i in range(SGU_GROUPS)]
    for n in range(ts // SGU_CHUNK):
        r0, r1 = n * SGU_CHUNK, (n + 1) * SGU_CHUNK
        s = jnp.concatenate(
            [jnp.dot(ws[i], vn[r0:r1, i * gd:(i + 1) * gd], preferred_element_type=F32)
             for i in range(SGU_GROUPS)], axis=1) + bs_ref[...]
        o_ref[r0:r1, :] = (u[r0:r1] * s).astype(o_ref.dtype)


def _sgu_mixer(p, ln_g, ln_b, w_s, b_s):
    B, S, _ = p.shape
    ts = min(SGU_TILE, S)
    gd = SGU_WIDTH // SGU_GROUPS
    bias = jnp.repeat(b_s.T, gd, axis=1)
    return pl.pallas_call(
        _sgu_kernel,
        grid=(B, S // ts),
        in_specs=[pl.BlockSpec((None, ts, 2 * SGU_WIDTH), lambda b, i: (b, i, 0)),
                  _const_spec((1, SGU_WIDTH)), _const_spec((1, SGU_WIDTH)),
                  _const_spec((SGU_GROUPS, SGU_CHUNK, SGU_CHUNK)), _const_spec((SGU_CHUNK, SGU_WIDTH))],
        out_specs=pl.BlockSpec((None, ts, SGU_WIDTH), lambda b, i: (b, i, 0)),
        out_shape=jax.ShapeDtypeStruct((B, S, SGU_WIDTH), BF16),
        compiler_params=_cparams("parallel", "parallel"),
        name="sgu",
    )(p, ln_g.reshape(1, -1), ln_b.reshape(1, -1), w_s, bias)


def _head_sum(x, bd_ref):
    return _dot_x2(x, bd_ref[...])


def _rwkv_prep_kernel(*refs, has_vres):
    if has_vres:
        (p_ref, mu_ref, w0_ref, w2_ref, a0_ref, a2_ref, g2_ref, kk_ref, ka_ref, bd_ref,
         vf_ref, v0_ref, v1_ref, v2_ref,
         r_o, ld_o, k_o, v_o, kn_o, ba_o, g_o, carry) = refs
    else:
        (p_ref, mu_ref, w0_ref, w2_ref, a0_ref, a2_ref, g2_ref, kk_ref, ka_ref, bd_ref,
         r_o, ld_o, k_o, v_o, kn_o, ba_o, g_o, carry) = refs
    W = RWKV_WIDTH
    p = p_ref[...]
    tr = p.shape[0]

    @pl.when(pl.program_id(1) == 0)
    def _():
        carry[...] = jnp.zeros_like(carry)

    rowi = lax.broadcasted_iota(jnp.int32, (tr, 1), 0)
    p_prev = jnp.where(rowi == 0, carry[0:1, :], pltpu.roll(p, 1, axis=0))
    carry[0:1, :] = p[tr - 1:tr, :]
    ps = p + (p_prev - p) * mu_ref[...]
    r, k, v = ps[:, 0:W], ps[:, W:2 * W], ps[:, 2 * W:3 * W]
    wa = ps[:, 3 * W:3 * W + DECAY_LORA + AAA_LORA]
    gd = ps[:, 3 * W + DECAY_LORA + AAA_LORA:]
    w = w0_ref[...] + _dot(jnp.tanh(wa), w2_ref[...])
    x = -w
    softplus = jnp.maximum(x, 0.0) + jnp.log(1.0 + jnp.exp(-jnp.abs(x)))
    ld_o[...] = -jnp.exp(-softplus - 0.5)
    a = _sigmoid(a0_ref[...] + _dot(wa, a2_ref[...]))
    g_o[...] = _dot(_sigmoid(gd), g2_ref[...])
    if has_vres:
        lora = _dot(_dot(v, v1_ref[...]), v2_ref[...])
        v = v + (vf_ref[...] - v) * _sigmoid(v0_ref[...] + lora)
    kk = k * kk_ref[...]
    norm = jnp.sqrt(_head_sum(kk * kk, bd_ref))
    kn = kk * (1.0 / jnp.maximum(norm, 1e-12))
    r_o[...] = r
    k_o[...] = k * (1.0 + (a - 1.0) * ka_ref[...])
    v_o[...] = v
    kn_o[...] = kn
    ba_o[...] = kn * a


def _head_block_diag():
    idx = np.arange(RWKV_WIDTH) // RWKV_HEAD_DIM
    return jnp.asarray(idx[:, None] == idx[None, :], BF16)


def _rwkv_prep(p, mu, w0, w2, a0, a2, g2, k_k, k_a, v_first, v_res):
    B, S, C = p.shape
    W = RWKV_WIDTH
    tr = min(RWKV_PREP_TILE, S)
    row = lambda z: z.reshape(1, -1)
    lora_rows = DECAY_LORA + AAA_LORA
    w2p = jnp.zeros((lora_rows, W), F32).at[:DECAY_LORA].set(w2).astype(BF16)
    a2p = jnp.zeros((lora_rows, W), F32).at[DECAY_LORA:].set(a2).astype(BF16)
    tok = lambda n: pl.BlockSpec((None, tr, n), lambda b, i: (b, i, 0))
    args = [p, row(mu), row(w0), w2p, row(a0), a2p, g2.astype(BF16), row(k_k), row(k_a), _head_block_diag()]
    specs = [tok(C), _const_spec((1, C)), _const_spec((1, W)), _const_spec((lora_rows, W)),
             _const_spec((1, W)), _const_spec((lora_rows, W)), _const_spec((GATE_LORA, W)),
             _const_spec((1, W)), _const_spec((1, W)), _const_spec((W, W))]
    if v_res is not None:
        v0, v1, v2 = v_res
        args += [v_first, row(v0), v1.astype(BF16), v2.astype(BF16)]
        specs += [tok(W), _const_spec((1, W)), _const_spec((W, MV_LORA)), _const_spec((MV_LORA, W))]
    return pl.pallas_call(
        functools.partial(_rwkv_prep_kernel, has_vres=v_res is not None),
        grid=(B, S // tr),
        in_specs=specs,
        out_specs=[tok(W)] * 7,
        out_shape=[jax.ShapeDtypeStruct((B, S, W), F32)] * 7,
        scratch_shapes=[pltpu.VMEM((8, C), F32)],
        compiler_params=_cparams("parallel", "arbitrary"),
        name="rwkv_prep",
    )(*args)


def _tri_inverse(n):
    size = n.shape[0]
    eye = (lax.broadcasted_iota(jnp.int32, (size, size), 0)
           == lax.broadcasted_iota(jnp.int32, (size, size), 1)).astype(F32)
    t = eye + n
    pw = n
    for _ in range(int(np.log2(size)) - 1):
        pw = _dot3(pw, pw)
        t = t + _dot3(t, pw)
    return t


def _rwkv_scan_kernel(r_ref, ld_ref, k_ref, v_ref, kn_ref, ba_ref, g_ref, rk_ref, lg_ref, lb_ref,
                      bd_ref, o_ref, state):
    L = r_ref.shape[0]
    N = RWKV_HEAD_DIM
    PW = 2 * N

    @pl.when(pl.program_id(1) == 0)
    def _():
        state[...] = jnp.zeros_like(state)

    ti = lax.broadcasted_iota(jnp.int32, (L, L), 0)
    si = lax.broadcasted_iota(jnp.int32, (L, L), 1)
    incl = ti >= si
    strict = ti > si
    tril_ones = jnp.where(incl, 1.0, 0.0).astype(BF16)
    lane = lax.broadcasted_iota(jnp.int32, (1, PW), 1)
    same_head = (lax.broadcasted_iota(jnp.int32, (PW, PW), 0) // N
                 == lax.broadcasted_iota(jnp.int32, (PW, PW), 1) // N)
    eye_pw = (lax.broadcasted_iota(jnp.int32, (PW, PW), 0)
              == lax.broadcasted_iota(jnp.int32, (PW, PW), 1)).astype(F32)

    ys = []
    for pr in range(RWKV_WIDTH // PW):
        sl = slice(pr * PW, (pr + 1) * PW)
        r, ld, k, v = r_ref[:, sl], ld_ref[:, sl], k_ref[:, sl], v_ref[:, sl]
        kn, ba = kn_ref[:, sl], ba_ref[:, sl]
        h1 = ld.astype(BF16)
        r1 = ld - h1.astype(F32)
        h2 = r1.astype(BF16)
        h3 = (r1 - h2.astype(F32)).astype(BF16)
        cum = _dot(tril_ones, h1) + (_dot(tril_ones, h2) + _dot(tril_ones, h3))
        w_in = jnp.exp(cum)
        w_inv = jnp.exp(-cum)
        rt = r * w_in
        kt = k * w_inv
        at = -kn * jnp.exp(cum - ld)
        bt = ba * w_inv
        w_last = w_in[L - 1:L, :]

        rhat, y0 = rt, jnp.zeros((L, PW), F32)
        ahat, uhat = jnp.zeros((L, PW), F32), jnp.zeros((L, PW), F32)
        for hd in range(2):
            hm = (lane // N) == hd
            at_h = jnp.where(hm, at, 0.0)
            rt_h = jnp.where(hm, rt, 0.0)
            v_h = jnp.where(hm, v, 0.0)
            a_ab = jnp.where(strict, _dot3(at_h, bt, _dot_nt), 0.0)
            a_ak = jnp.where(strict, _dot3(at_h, kt, _dot_nt), 0.0)
            m_rb = jnp.where(incl, _dot3(rt_h, bt, _dot_nt), 0.0)
            m_rk = jnp.where(incl, _dot3(rt_h, kt, _dot_nt), 0.0)
            t_inv = _tri_inverse(a_ab)
            ahat_h = _dot3(t_inv, at_h)
            uhat_h = _dot3(t_inv, _dot3(a_ak, v_h))
            rhat = rhat + _dot3(m_rb, ahat_h)
            y0 = y0 + (_dot3(m_rb, uhat_h) + _dot3(m_rk, v_h))
            ahat = ahat + ahat_h
            uhat = uhat + uhat_h
        p_mat = jnp.where(same_head, eye_pw + _dot3(ahat, bt, _dot_tn), 0.0) * w_last
        q_mat = jnp.where(same_head, _dot3(uhat, bt, _dot_tn) + _dot3(v, kt, _dot_tn), 0.0) * w_last

        s_old = state[pr]
        ys.append(_dot3(rhat, s_old, _dot_nt) + y0)
        state[pr] = _dot3(s_old, p_mat) + q_mat

    y = jnp.concatenate(ys, axis=1)
    inv_n = 1.0 / N
    mean = _head_sum(y, bd_ref) * inv_n
    yc = y - mean
    var = _head_sum(yc * yc, bd_ref) * inv_n
    yn = yc * lax.rsqrt(var + LNX_EPS) * lg_ref[...] + lb_ref[...]
    r, k, v = r_ref[...], k_ref[...], v_ref[...]
    bonus = _head_sum(r * k * rk_ref[...], bd_ref) * v
    o_ref[...] = ((yn + bonus) * g_ref[...]).astype(o_ref.dtype)


def _rwkv_scan(r, ld, k, v, kn, ba, g, r_k, lnx_g, lnx_b):
    B, S, W = r.shape
    L = min(RWKV_CHUNK, S)
    tok = pl.BlockSpec((None, L, W), lambda b, i: (b, i, 0))
    row = lambda z: z.reshape(1, -1)
    return pl.pallas_call(
        _rwkv_scan_kernel,
        grid=(B, S // L),
        in_specs=[tok] * 7 + [_const_spec((1, W))] * 3 + [_const_spec((W, W))],
        out_specs=tok,
        out_shape=jax.ShapeDtypeStruct((B, S, W), BF16),
        scratch_shapes=[pltpu.VMEM((W // (2 * RWKV_HEAD_DIM), 2 * RWKV_HEAD_DIM, 2 * RWKV_HEAD_DIM), F32)],
        compiler_params=_cparams("parallel", "arbitrary"),
        name="rwkv_scan",
    )(r, ld, k, v, kn, ba, g, row(r_k), row(lnx_g), row(lnx_b), _head_block_diag())


def _rwkv_mixer(p, mu, w0, w2, a0, a2, g2, k_k, k_a, r_k, lnx_g, lnx_b, v_first, v_res):
    r, ld, k, v, kn, ba, g = _rwkv_prep(p, mu, w0, w2, a0, a2, g2, k_k, k_a, v_first, v_res)
    y = _rwkv_scan(r, ld, k, v, kn, ba, g, r_k, lnx_g, lnx_b)
    return y, (v if v_res is None else v_first)


def _merge_kernel(x_ref, mod_ref, pg_ref, qg_ref, ya_ref, yb_ref, yc_ref, wg_ref, bw_ref, ow_ref, o_ref):
    x = x_ref[...]
    D = x.shape[1]
    h = _pre(x, pg_ref[...], mod_ref).astype(BF16)
    merged = None
    for i, y_ref in enumerate((ya_ref, yb_ref, yc_ref)):
        gate = _sigmoid(jnp.dot(h, wg_ref[:, i * D:(i + 1) * D], preferred_element_type=F32))
        term = gate * jnp.dot(y_ref[...], bw_ref[i], preferred_element_type=F32)
        merged = term if merged is None else merged + term
    y = _dot(merged, ow_ref[...])
    o_ref[...] = x + mod_ref[2:3, :] * _rms(y, qg_ref[...])


def _merge(x, mod3, pre_g, post_g, y_a, y_b, y_c, w_gate, branch_w, out_w):
    B, S, D = x.shape
    tm = min(FFN_TILE, S)
    tok = lambda n: pl.BlockSpec((None, tm, n), lambda b, i: (b, i, 0))
    return pl.pallas_call(
        _merge_kernel,
        grid=(B, S // tm),
        in_specs=[tok(D), pl.BlockSpec((None, 3, D), lambda b, i: (b, 0, 0)),
                  _const_spec((1, D)), _const_spec((1, D)),
                  tok(BRANCH_WIDTH), tok(BRANCH_WIDTH), tok(BRANCH_WIDTH),
                  _const_spec((D, N_BRANCH * D)), _const_spec((N_BRANCH, BRANCH_WIDTH, D)),
                  _const_spec((D, D))],
        out_specs=tok(D),
        out_shape=jax.ShapeDtypeStruct((B, S, D), F32),
        compiler_params=_cparams("parallel", "parallel"),
        name="merge",
    )(x, mod3, pre_g.reshape(1, D), post_g.reshape(1, D), y_a, y_b, y_c, w_gate, branch_w, out_w)


def kernel(x, c, ada_w, ada_b, pre_g, post_g, ffn_w_in, ffn_w_out, mix_w_in, branch_w, out_w,
           cmp_k_w1, cmp_k_w2, cmp_k_pe, cmp_v_w1, cmp_v_w2, cmp_v_pe,
           sgu_ln_g, sgu_ln_b, sgu_w, sgu_b,
           rwkv_mu, rwkv_w0, rwkv_w2, rwkv_a0, rwkv_a2, rwkv_g2, rwkv_kk, rwkv_ka, rwkv_rk,
           rwkv_lnx_g, rwkv_lnx_b, rwkv_v0, rwkv_v1, rwkv_v2):
    B, S, D = x.shape
    depth = ada_w.shape[0]
    mod = _ada_mod(c, ada_w, ada_b).reshape(depth, B, 3, 3, D)
    v_first = None
    for l in range(depth):
        x = _ffn(x, mod[l, :, 0], pre_g[l, 0], post_g[l, 0],
                 ffn_w_in[l, 0].astype(BF16), ffn_w_out[l, 0].astype(BF16))

        w = mix_w_in[l]
        n_in = NSA_COLS + 2 * SGU_WIDTH + RWKV_COLS
        w_proj = jnp.concatenate(
            [w[:, :Q_COLS + KV_COLS],
             jnp.pad(w[:, Q_COLS + KV_COLS:NSA_COLS], ((0, 0), (0, LANES - GT_COLS))),
             w[:, NSA_COLS:n_in]], axis=1).astype(BF16)
        q, kv, gt, p_sgu, p_rwkv = _mix_proj(x, mod[l, :, 1], pre_g[l, 1], w_proj)
        y_a = _nsa_mixer(q, kv, gt, cmp_k_w1[l], cmp_k_w2[l], cmp_k_pe[l],
                         cmp_v_w1[l], cmp_v_w2[l], cmp_v_pe[l])
        y_b = _sgu_mixer(p_sgu, sgu_ln_g[l], sgu_ln_b[l], sgu_w[l], sgu_b[l])
        v_res = None if l == 0 else (rwkv_v0[l - 1], rwkv_v1[l - 1], rwkv_v2[l - 1])
        y_c, v_first = _rwkv_mixer(p_rwkv, rwkv_mu[l], rwkv_w0[l], rwkv_w2[l], rwkv_a0[l], rwkv_a2[l],
                                   rwkv_g2[l], rwkv_kk[l], rwkv_ka[l], rwkv_rk[l],
                                   rwkv_lnx_g[l], rwkv_lnx_b[l], v_first, v_res)
        x = _merge(x, mod[l, :, 1], pre_g[l, 1], post_g[l, 1], y_a, y_b, y_c,
                   w[:, n_in:].astype(BF16), branch_w[l].astype(BF16), out_w[l].astype(BF16))

        x = _ffn(x, mod[l, :, 2], pre_g[l, 2], post_g[l, 2],
                 ffn_w_in[l, 1].astype(BF16), ffn_w_out[l, 1].astype(BF16))
    return x
```

```python
import functools

import numpy as np
import jax
import jax.numpy as jnp
from jax import lax
from jax.experimental import pallas as pl
from jax.experimental.pallas import tpu as pltpu

D_MODEL = 1024
DEPTH = 2
NSA_HEADS = 8
NSA_GROUPS = 2
NSA_HPG = NSA_HEADS // NSA_GROUPS
NSA_DK = 64
CMP_LEN = 32
CMP_STRIDE = 16
CMP_HID = 256
SLC_BLOCK = 64
SLC_TOPN = 16
WINDOW = 512
Q_BLOCK = 128
SGU_CHUNK = 128
SGU_GROUPS = 4
SGU_WIDTH = 512
RWKV_HEADS = 8
RWKV_HEAD_DIM = 64
RWKV_WIDTH = RWKV_HEADS * RWKV_HEAD_DIM
DECAY_LORA = 64
AAA_LORA = 64
MV_LORA = 32
GATE_LORA = 128
N_BRANCH = 3
BRANCH_WIDTH = 512
D_FF = 2816
MACARON_WEIGHT = 0.5
RMS_EPS = 1e-6
LN_EPS = 1e-5
LNX_EPS = 64e-5
NEG_INF = -1e30
FORCE_SCORE = 1e4

NSA_COLS = NSA_HEADS * NSA_DK + 6 * NSA_GROUPS * NSA_DK + 3 * NSA_HEADS
RWKV_COLS = 3 * RWKV_WIDTH + DECAY_LORA + AAA_LORA + GATE_LORA
Q_COLS = NSA_HEADS * NSA_DK
KV_COLS = 6 * NSA_GROUPS * NSA_DK
GT_COLS = 3 * NSA_HEADS
LANES = 128
NSA_AUG = 128
POS_SPLIT = 64
MASK_BIG = 2.0 ** 100

BF16 = jnp.bfloat16
F32 = jnp.float32

FFN_TILE = 512
FFN_CHUNK = 1408
SEL_TILE = 512
RWKV_CHUNK = 64
RWKV_STEP = 128
RWKV_PREP_TILE = 256
SGU_TILE = 512
VMEM_LIMIT = 56 * 1024 * 1024


def _cparams(*sem):
    return pltpu.CompilerParams(dimension_semantics=sem, vmem_limit_bytes=VMEM_LIMIT)


def _dot(a, b):
    return jnp.dot(a.astype(BF16), b.astype(BF16), preferred_element_type=F32)


def _dot_nt(a, b):
    return lax.dot_general(a.astype(BF16), b.astype(BF16), (((1,), (1,)), ((), ())),
                           preferred_element_type=F32)


def _dot_tn(a, b):
    return lax.dot_general(a.astype(BF16), b.astype(BF16), (((0,), (0,)), ((), ())),
                           preferred_element_type=F32)


def _split(x):
    hi = x.astype(BF16)
    lo = (x - hi.astype(F32)).astype(BF16)
    return hi, lo


def _dot_x2(a, b):
    hi, lo = _split(a)
    return _dot(hi, b) + _dot(lo, b)


def _sigmoid(x):
    return 1.0 / (1.0 + jnp.exp(-x))


def _silu(x):
    return x * _sigmoid(x)


def _rms(x, g):
    return x * lax.rsqrt(jnp.mean(x * x, axis=-1, keepdims=True) + RMS_EPS) * g


def _pre(x, g, mod_ref):
    return _rms(x, g) * (1.0 + mod_ref[1:2, :]) + mod_ref[0:1, :]


def _ada_kernel(c_ref, w_ref, b_ref, o_ref):
    cond = _silu(c_ref[...])
    o_ref[...] = _dot(cond, w_ref[...]) + b_ref[...]


def _ada_mod(c, ada_w, ada_b):
    L, D, N = ada_w.shape
    B = c.shape[0]
    tn = 1536
    return pl.pallas_call(
        _ada_kernel,
        grid=(L, N // tn),
        in_specs=[
            pl.BlockSpec((B, D), lambda l, j: (0, 0)),
            pl.BlockSpec((None, D, tn), lambda l, j: (l, 0, j)),
            pl.BlockSpec((None, 1, tn), lambda l, j: (l, 0, j)),
        ],
        out_specs=pl.BlockSpec((None, B, tn), lambda l, j: (l, 0, j)),
        out_shape=jax.ShapeDtypeStruct((L, B, N), F32),
        compiler_params=_cparams("arbitrary", "arbitrary"),
        name="ada_mod",
    )(c, ada_w, ada_b.reshape(L, 1, N))


def _ffn_kernel(x_ref, mod_ref, pg_ref, qg_ref, win_ref, wout_ref, o_ref):
    x = x_ref[...]
    h = _pre(x, pg_ref[...], mod_ref).astype(BF16)
    acc = None
    for c in range(D_FF // FFN_CHUNK):
        lo, hi = c * FFN_CHUNK, (c + 1) * FFN_CHUNK
        gate = jnp.dot(h, win_ref[:, lo:hi], preferred_element_type=F32)
        up = jnp.dot(h, win_ref[:, D_FF + lo:D_FF + hi], preferred_element_type=F32)
        act = (_silu(gate) * up).astype(BF16)
        part = jnp.dot(act, wout_ref[lo:hi, :], preferred_element_type=F32)
        acc = part if acc is None else acc + part
    o_ref[...] = x + MACARON_WEIGHT * mod_ref[2:3, :] * _rms(acc, qg_ref[...])


def _const_spec(shape):
    nd = len(shape)
    return pl.BlockSpec(shape, lambda *_: (0,) * nd, pipeline_mode=pl.Buffered(1))


def _ffn(x, mod3, pre_g, post_g, w_in, w_out):
    B, S, D = x.shape
    tm = min(FFN_TILE, S)
    return pl.pallas_call(
        _ffn_kernel,
        grid=(B, S // tm),
        in_specs=[
            pl.BlockSpec((None, tm, D), lambda b, i: (b, i, 0)),
            pl.BlockSpec((None, 3, D), lambda b, i: (b, 0, 0)),
            _const_spec((1, D)),
            _const_spec((1, D)),
            _const_spec((D, 2 * D_FF)),
            _const_spec((D_FF, D)),
        ],
        out_specs=pl.BlockSpec((None, tm, D), lambda b, i: (b, i, 0)),
        out_shape=jax.ShapeDtypeStruct((B, S, D), F32),
        compiler_params=_cparams("parallel", "parallel"),
        name="ffn",
    )(x, mod3, pre_g.reshape(1, D), post_g.reshape(1, D), w_in, w_out)


PROJ_COLS = Q_COLS + KV_COLS + LANES + 2 * SGU_WIDTH + RWKV_COLS


def _proj_kernel(x_ref, mod_ref, pg_ref, w_ref, q_ref, kv_ref, gt_ref, sgu_ref, rw_ref):
    h = _pre(x_ref[...], pg_ref[...], mod_ref).astype(BF16)
    c0, c1, c2, c3 = Q_COLS, Q_COLS + KV_COLS, Q_COLS + KV_COLS + LANES, PROJ_COLS - RWKV_COLS
    q = jnp.dot(h, w_ref[:, 0:c0], preferred_element_type=F32)
    q_ref[...] = (q * NSA_DK ** -0.5).astype(BF16)
    kv_ref[...] = jnp.dot(h, w_ref[:, c0:c1], preferred_element_type=F32).astype(BF16)
    gt_ref[...] = jnp.dot(h, w_ref[:, c1:c2], preferred_element_type=F32)
    sgu_ref[...] = jnp.dot(h, w_ref[:, c2:c3], preferred_element_type=F32)
    rw_ref[...] = jnp.dot(h, w_ref[:, c3:PROJ_COLS], preferred_element_type=F32)


def _mix_proj(x, mod3, pre_g, w):
    B, S, D = x.shape
    tm = min(FFN_TILE, S)
    widths = (Q_COLS, KV_COLS, LANES, 2 * SGU_WIDTH, RWKV_COLS)
    dtypes = (BF16, BF16, F32, F32, F32)
    return pl.pallas_call(
        _proj_kernel,
        grid=(B, S // tm),
        in_specs=[
            pl.BlockSpec((None, tm, D), lambda b, i: (b, i, 0)),
            pl.BlockSpec((None, 3, D), lambda b, i: (b, 0, 0)),
            _const_spec((1, D)),
            _const_spec((D, PROJ_COLS)),
        ],
        out_specs=[pl.BlockSpec((None, tm, n), lambda b, i: (b, i, 0)) for n in widths],
        out_shape=[jax.ShapeDtypeStruct((B, S, n), dt) for n, dt in zip(widths, dtypes)],
        compiler_params=_cparams("parallel", "parallel"),
        name="mix_proj",
    )(x, mod3, pre_g.reshape(1, D), w)


def _cmp_kernel(xk_ref, xv_ref, pek_ref, pev_ref, w1k_ref, w1v_ref, w2kt_ref, w2v_ref,
                kct_ref, vc_ref):
    half = (CMP_LEN // 2) * NSA_DK

    def hidden(x_ref, pe_ref, w1_ref):
        x = x_ref[...].astype(F32)
        n = x.shape[0]
        top = _dot(x + pe_ref[0:1, :], w1_ref[0:half, :])
        bot = _dot(x + pe_ref[1:2, :], w1_ref[half:2 * half, :])
        return _silu(top + pltpu.roll(bot, n - 1, axis=0))

    kct_ref[...] = _dot_nt(w2kt_ref[...], hidden(xk_ref, pek_ref, w1k_ref)).astype(BF16)
    vc_ref[...] = _dot(hidden(xv_ref, pev_ref, w1v_ref), w2v_ref[...]).astype(BF16)


def _compress(xk, xv, pe_k, pe_v, w1k, w1v, w2kt, w2v):
    B, G, NC, F = xk.shape
    blk = lambda *s: pl.BlockSpec((None, None) + s, lambda b, g: (b, g, 0, 0))
    return pl.pallas_call(
        _cmp_kernel,
        grid=(B, G),
        in_specs=[blk(NC, F), blk(NC, F), _const_spec((2, F)), _const_spec((2, F)),
                  _const_spec((2 * F, CMP_HID)), _const_spec((2 * F, CMP_HID)),
                  _const_spec((NSA_DK, CMP_HID)), _const_spec((CMP_HID, NSA_DK))],
        out_specs=[blk(NSA_DK, NC), blk(NC, NSA_DK)],
        out_shape=[jax.ShapeDtypeStruct((B, G, NSA_DK, NC), BF16),
                   jax.ShapeDtypeStruct((B, G, NC, NSA_DK), BF16)],
        compiler_params=_cparams("parallel", "parallel"),
        name="nsa_compress",
    )(xk, xv, pe_k, pe_v, w1k, w1v, w2kt, w2v)


def _masked_softmax(s, valid):
    s = jnp.where(valid, s, NEG_INF)
    m = jnp.max(s, axis=-1, keepdims=True)
    p = jnp.where(valid, jnp.exp(s - m), 0.0)
    l = jnp.sum(p, axis=-1, keepdims=True)
    return p * (1.0 / jnp.where(l > 0.0, l, 1.0))


def _nsa_kernel(q_ref, gt_ref, kc_ref, vc_ref, ks_ref, vs_ref, kw_ref, vw_ref, selt_ref, e_ref,
                o_ref, imp_ref, *, n_top, tk):
    qb = pl.program_id(2)
    s0 = qb * Q_BLOCK
    R = NSA_HPG * Q_BLOCK
    n_slc = selt_ref.shape[0]
    n_cmp = kc_ref.shape[-1]
    q = q_ref[...].reshape(R, NSA_AUG)
    t_i = s0 + (lax.broadcasted_iota(jnp.int32, (R, 1), 0) & (Q_BLOCK - 1))

    s_c = jnp.dot(q, kc_ref[...], preferred_element_type=F32)
    cmp_end = lax.broadcasted_iota(jnp.int32, (1, n_cmp), 1) * CMP_STRIDE + (CMP_LEN - 1)
    p_c = _masked_softmax(s_c, cmp_end <= t_i)
    o_c = _dot(p_c, vc_ref[...])

    p_sum = p_c[0:Q_BLOCK]
    for h in range(1, NSA_HPG):
        p_sum = p_sum + p_c[h * Q_BLOCK:(h + 1) * Q_BLOCK]
    hi, lo = _split(p_sum)
    imp = _dot_nt(selt_ref[...], hi) + _dot_nt(selt_ref[...], lo)
    j = lax.broadcasted_iota(jnp.int32, (n_slc, Q_BLOCK), 0)
    tq = s0 + lax.broadcasted_iota(jnp.int32, (n_slc, Q_BLOCK), 1)
    cur = tq // SLC_BLOCK
    forced = (j == 0) | (j == cur) | (j == cur - 1)
    live = j * SLC_BLOCK <= tq
    imp = jnp.where(forced, FORCE_SCORE, jnp.where(live, imp, NEG_INF))
    imp_ref[...] = imp

    def rank_body(i, cnt):
        r = imp_ref[pl.ds(i, 1), :]
        tie = jnp.where(j > i, 1.0, 0.0)
        return cnt + jnp.where(r > imp, 1.0, jnp.where(r == imp, tie, 0.0))

    n_live = (s0 + Q_BLOCK) // SLC_BLOCK
    cnt = lax.fori_loop(0, n_live, rank_body, jnp.zeros((n_slc, Q_BLOCK), F32))
    dropped = jnp.concatenate(
        [jnp.where(cnt < n_top, 0.0, 1.0), jnp.zeros((NSA_AUG - n_slc, Q_BLOCK), F32)], axis=0).T.astype(BF16)
    lhs = jnp.concatenate([q, jnp.concatenate([dropped] * NSA_HPG, axis=0)], axis=1)

    def sel_tile(kt, carry, diagonal):
        m, l, acc = carry
        rhs = jnp.concatenate([ks_ref[kt], e_ref[kt]], axis=0)
        s = jnp.dot(lhs, rhs, preferred_element_type=F32)
        if diagonal:
            pos = kt * tk + lax.broadcasted_iota(jnp.int32, (1, tk), 1)
            s = jnp.where(pos <= t_i, s, NEG_INF)
        m_new = jnp.maximum(m, jnp.max(s, axis=-1, keepdims=True))
        alpha = jnp.exp(m - m_new)
        p = jnp.exp(s - m_new)
        l = alpha * l + jnp.sum(p, axis=-1, keepdims=True)
        v = vs_ref[pl.ds(pl.multiple_of(kt * tk, tk), tk), :]
        acc = alpha * acc + _dot(p, v)
        return m_new, l, acc

    last = (s0 + Q_BLOCK - 1) // tk
    init = (jnp.full((R, 1), NEG_INF, F32), jnp.zeros((R, 1), F32), jnp.zeros((R, NSA_DK), F32))
    carry = lax.fori_loop(0, last, functools.partial(sel_tile, diagonal=False), init)
    _, l_s, acc_s = sel_tile(last, carry, True)
    o_s = acc_s * (1.0 / l_s)

    start = jnp.maximum(s0 - WINDOW, 0)
    wt0 = start // LANES
    n_wt = (WINDOW + Q_BLOCK) // LANES
    s_w = jnp.concatenate(
        [jnp.dot(q, kw_ref[wt0 + i], preferred_element_type=F32) for i in range(n_wt)], axis=1)
    pos_w = start + lax.broadcasted_iota(jnp.int32, (1, n_wt * LANES), 1)
    s_w = jnp.where(pos_w <= t_i, jnp.where(pos_w > t_i - WINDOW, s_w, NEG_INF), NEG_INF)
    p_w = jnp.exp(s_w - jnp.max(s_w, axis=-1, keepdims=True))
    p_w = p_w * (1.0 / jnp.sum(p_w, axis=-1, keepdims=True))
    o_w = _dot(p_w, vw_ref[pl.ds(pl.multiple_of(start, LANES), n_wt * LANES), :])

    gate = _sigmoid(gt_ref[...].reshape(R, 3))
    o = gate[:, 0:1] * o_c + gate[:, 1:2] * o_s + gate[:, 2:3] * o_w
    o_ref[...] = o.reshape(NSA_HPG, Q_BLOCK, NSA_DK).astype(o_ref.dtype)


def _selection_map_t(n_cmp_rows, n_slc):
    rs, rc = SLC_BLOCK // CMP_STRIDE, CMP_LEN // CMP_STRIDE
    m = np.zeros((n_slc, n_cmp_rows), np.float32)
    for jj in range(n_slc):
        for a in range(rs):
            for b in range(rc):
                i = rs * jj - a - b
                if 0 <= i < n_cmp_rows - 1:
                    m[jj, i] += 1.0
    return jnp.asarray(m, BF16)


def _nsa_attention(q, gt, kc, vc, ks, vs, kw, vw):
    B, H, S, A = q.shape
    G = NSA_GROUPS
    n_slc = S // SLC_BLOCK
    n_cmp = kc.shape[-1]
    n_kt, _, tk = ks.shape[2:]
    selt = _selection_map_t(n_cmp, n_slc)
    key_blk = np.arange(S).reshape(n_kt, 1, tk) // SLC_BLOCK
    e = np.where(np.arange(A).reshape(1, A, 1) == key_blk, -MASK_BIG, 0.0)
    per_q = lambda d: pl.BlockSpec((None, NSA_HPG, Q_BLOCK, d), lambda b, g, i: (b, g, i, 0))
    per_g = lambda *s: pl.BlockSpec((None, None) + s, lambda b, g, i: (b, g) + (0,) * len(s))
    kern = functools.partial(_nsa_kernel, n_top=min(SLC_TOPN, n_slc), tk=tk)
    return pl.pallas_call(
        kern,
        grid=(B, G, S // Q_BLOCK),
        in_specs=[per_q(A), per_q(3), per_g(A, n_cmp), per_g(n_cmp, NSA_DK),
                  per_g(n_kt, A, tk), per_g(S, NSA_DK), per_g(S // LANES, A, LANES), per_g(S, NSA_DK),
                  pl.BlockSpec((n_slc, n_cmp), lambda b, g, i: (0, 0)),
                  pl.BlockSpec((n_kt, A, tk), lambda b, g, i: (0, 0, 0))],
        out_specs=per_q(NSA_DK),
        out_shape=jax.ShapeDtypeStruct((B, H, S, NSA_DK), BF16),
        scratch_shapes=[pltpu.VMEM((n_slc, Q_BLOCK), F32)],
        compiler_params=_cparams("parallel", "parallel", "arbitrary"),
        name="nsa_attention",
    )(q, gt, kc, vc, ks, vs, kw, vw, selt, jnp.asarray(e, BF16))


def _with_position_rows(kt, pos):
    lead = kt.shape[:-2]
    n = kt.shape[-1]
    hi = jnp.broadcast_to(jnp.asarray(pos // POS_SPLIT, BF16)[..., None, :], lead + (1, n))
    lo = jnp.broadcast_to(jnp.asarray(pos % POS_SPLIT, BF16)[..., None, :], lead + (1, n))
    pad = jnp.zeros(lead + (NSA_AUG - NSA_DK - 2, n), BF16)
    return jnp.concatenate([kt, hi, lo, pad], axis=-2)


def _nsa_mixer(q, kv, gt, k_w1, k_w2, k_pe, v_w1, v_w2, v_pe):
    B, S, _ = q.shape
    G, DK = NSA_GROUPS, NSA_DK
    kc, vc, ks, vs, kw, vw = [kv[:, :, i * G * DK:(i + 1) * G * DK].reshape(B, S, G, DK) for i in range(6)]
    half = CMP_LEN // 2
    nc = S // half

    def half_blocks(z):
        return z.reshape(B, nc, half, G, DK).transpose(0, 3, 1, 2, 4).reshape(B, G, nc, half * DK)

    kct, vcc = _compress(
        half_blocks(kc), half_blocks(vc), k_pe.reshape(2, half * DK), v_pe.reshape(2, half * DK),
        k_w1.astype(BF16), v_w1.astype(BF16), k_w2.T.astype(BF16), v_w2.astype(BF16))

    def key_tiles(z, t):
        tiles = z.reshape(B, S // t, t, G, DK).transpose(0, 3, 1, 4, 2)
        return _with_position_rows(tiles, np.arange(S).reshape(S // t, t))

    tk = min(SEL_TILE, S)
    slopes = 2.0 ** -np.arange(1, NSA_HEADS + 1, dtype=np.float32)
    q_cols = np.zeros((NSA_HEADS, NSA_AUG - DK), np.float32)
    q_cols[:, 0] = POS_SPLIT * slopes
    q_cols[:, 1] = slopes
    qh = q.reshape(B, S, NSA_HEADS, DK).transpose(0, 2, 1, 3)
    qh = jnp.concatenate(
        [qh, jnp.broadcast_to(jnp.asarray(q_cols, BF16)[None, :, None, :], (B, NSA_HEADS, S, NSA_AUG - DK))], axis=-1)
    gth = gt[:, :, :GT_COLS].reshape(B, S, NSA_HEADS, 3).transpose(0, 2, 1, 3)
    kca = _with_position_rows(kct, np.arange(nc) * CMP_STRIDE + (CMP_LEN - 1))
    o = _nsa_attention(qh, gth, kca, vcc, key_tiles(ks, tk), vs.transpose(0, 2, 1, 3),
                       key_tiles(kw, LANES), vw.transpose(0, 2, 1, 3))
    return o.transpose(0, 2, 1, 3).reshape(B, S, NSA_HEADS * DK)


def _gelu_tanh(x):
    return 0.5 * x * (1.0 + jnp.tanh(np.sqrt(2.0 / np.pi).astype(np.float32) * (x + 0.044715 * (x * x * x))))


def _sgu_kernel(p_ref, g_ref, b_ref, w_ref, bs_ref, o_ref):
    ts = p_ref.shape[0]
    gd = SGU_WIDTH // SGU_GROUPS
    u = _gelu_tanh(p_ref[:, 0:SGU_WIDTH])
    v = _gelu_tanh(p_ref[:, SGU_WIDTH:2 * SGU_WIDTH])
    mu = jnp.mean(v, axis=-1, keepdims=True)
    vc = v - mu
    var = jnp.mean(vc * vc, axis=-1, keepdims=True)
    vn = (vc * lax.rsqrt(var + LN_EPS) * g_ref[...] + b_ref[...]).astype(BF16)
    causal = (lax.broadcasted_iota(jnp.int32, (SGU_CHUNK, SGU_CHUNK), 0)
              >= lax.broadcasted_iota(jnp.int32, (SGU_CHUNK, SGU_CHUNK), 1))
    ws = [jnp.where(causal, w_ref[i], 0.0).astype(BF16) for i in range(SGU_GROUPS)]
    for n in range(ts // SGU_CHUNK):
        r0, r1 = n * SGU_CHUNK, (n + 1) * SGU_CHUNK
        s = jnp.concatenate(
            [jnp.dot(ws[i], vn[r0:r1, i * gd:(i + 1) * gd], preferred_element_type=F32)
             for i in range(SGU_GROUPS)], axis=1) + bs_ref[...]
        o_ref[r0:r1, :] = (u[r0:r1] * s).astype(o_ref.dtype)


def _sgu_mixer(p, ln_g, ln_b, w_s, b_s):
    B, S, _ = p.shape
    ts = min(SGU_TILE, S)
    gd = SGU_WIDTH // SGU_GROUPS
    bias = jnp.repeat(b_s.T, gd, axis=1)
    return pl.pallas_call(
        _sgu_kernel,
        grid=(B, S // ts),
        in_specs=[pl.BlockSpec((None, ts, 2 * SGU_WIDTH), lambda b, i: (b, i, 0)),
                  _const_spec((1, SGU_WIDTH)), _const_spec((1, SGU_WIDTH)),
                  _const_spec((SGU_GROUPS, SGU_CHUNK, SGU_CHUNK)), _const_spec((SGU_CHUNK, SGU_WIDTH))],
        out_specs=pl.BlockSpec((None, ts, SGU_WIDTH), lambda b, i: (b, i, 0)),
        out_shape=jax.ShapeDtypeStruct((B, S, SGU_WIDTH), BF16),
        compiler_params=_cparams("parallel", "parallel"),
        name="sgu",
    )(p, ln_g.reshape(1, -1), ln_b.reshape(1, -1), w_s, bias)


def _head_sum(x, bd_ref):
    return _dot_x2(x, bd_ref[...])


def _rwkv_prep_kernel(*refs, has_vres):
    if has_vres:
        (p_ref, mu_ref, w0_ref, w2_ref, a0_ref, a2_ref, g2_ref, kk_ref, ka_ref, bd_ref,
         vf_ref, v0_ref, v1_ref, v2_ref,
         r_o, ld_o, k_o, v_o, kn_o, ba_o, g_o, carry) = refs
    else:
        (p_ref, mu_ref, w0_ref, w2_ref, a0_ref, a2_ref, g2_ref, kk_ref, ka_ref, bd_ref,
         r_o, ld_o, k_o, v_o, kn_o, ba_o, g_o, carry) = refs
    W = RWKV_WIDTH
    p = p_ref[...]
    tr = p.shape[0]

    @pl.when(pl.program_id(1) == 0)
    def _():
        carry[...] = jnp.zeros_like(carry)

    rowi = lax.broadcasted_iota(jnp.int32, (tr, 1), 0)
    p_prev = jnp.where(rowi == 0, carry[0:1, :], pltpu.roll(p, 1, axis=0))
    carry[0:1, :] = p[tr - 1:tr, :]
    ps = p + (p_prev - p) * mu_ref[...]
    r, k, v = ps[:, 0:W], ps[:, W:2 * W], ps[:, 2 * W:3 * W]
    wa = ps[:, 3 * W:3 * W + DECAY_LORA + AAA_LORA]
    gd = ps[:, 3 * W + DECAY_LORA + AAA_LORA:]
    w = w0_ref[...] + _dot(jnp.tanh(wa), w2_ref[...])
    x = -w
    softplus = jnp.maximum(x, 0.0) + jnp.log(1.0 + jnp.exp(-jnp.abs(x)))
    ld_o[...] = -jnp.exp(-softplus - 0.5)
    a = _sigmoid(a0_ref[...] + _dot(wa, a2_ref[...]))
    g_o[...] = _dot(_sigmoid(gd), g2_ref[...])
    if has_vres:
        lora = _dot(_dot(v, v1_ref[...]), v2_ref[...])
        v = v + (vf_ref[...] - v) * _sigmoid(v0_ref[...] + lora)
    kk = k * kk_ref[...]
    norm = jnp.sqrt(_head_sum(kk * kk, bd_ref))
    kn = kk * (1.0 / jnp.maximum(norm, 1e-12))
    r_o[...] = r
    k_o[...] = k * (1.0 + (a - 1.0) * ka_ref[...])
    v_o[...] = v
    kn_o[...] = kn
    ba_o[...] = kn * a


def _head_block_diag():
    idx = np.arange(RWKV_WIDTH) // RWKV_HEAD_DIM
    return jnp.asarray(idx[:, None] == idx[None, :], BF16)


def _rwkv_prep(p, mu, w0, w2, a0, a2, g2, k_k, k_a, v_first, v_res):
    B, S, C = p.shape
    W = RWKV_WIDTH
    tr = min(RWKV_PREP_TILE, S)
    row = lambda z: z.reshape(1, -1)
    lora_rows = DECAY_LORA + AAA_LORA
    w2p = jnp.zeros((lora_rows, W), F32).at[:DECAY_LORA].set(w2).astype(BF16)
    a2p = jnp.zeros((lora_rows, W), F32).at[DECAY_LORA:].set(a2).astype(BF16)
    tok = lambda n: pl.BlockSpec((None, tr, n), lambda b, i: (b, i, 0))
    args = [p, row(mu), row(w0), w2p, row(a0), a2p, g2.astype(BF16), row(k_k), row(k_a), _head_block_diag()]
    specs = [tok(C), _const_spec((1, C)), _const_spec((1, W)), _const_spec((lora_rows, W)),
             _const_spec((1, W)), _const_spec((lora_rows, W)), _const_spec((GATE_LORA, W)),
             _const_spec((1, W)), _const_spec((1, W)), _const_spec((W, W))]
    if v_res is not None:
        v0, v1, v2 = v_res
        args += [v_first, row(v0), v1.astype(BF16), v2.astype(BF16)]
        specs += [tok(W), _const_spec((1, W)), _const_spec((W, MV_LORA)), _const_spec((MV_LORA, W))]
    return pl.pallas_call(
        functools.partial(_rwkv_prep_kernel, has_vres=v_res is not None),
        grid=(B, S // tr),
        in_specs=specs,
        out_specs=[tok(W)] * 7,
        out_shape=[jax.ShapeDtypeStruct((B, S, W), F32)] * 7,
        scratch_shapes=[pltpu.VMEM((8, C), F32)],
        compiler_params=_cparams("parallel", "arbitrary"),
        name="rwkv_prep",
    )(*args)


def _rwkv_scan_kernel(r_ref, ld_ref, k_ref, v_ref, kn_ref, ba_ref, g_ref, rk_ref, lg_ref, lb_ref,
                      bd_ref, o_ref, state):
    L = RWKV_CHUNK
    N = RWKV_HEAD_DIM
    PW = 2 * N
    SR = 2 * L
    n_sub = r_ref.shape[0] // L

    @pl.when(pl.program_id(1) == 0)
    def _():
        state[...] = jnp.zeros_like(state)

    tt = lax.broadcasted_iota(jnp.int32, (n_sub * L, n_sub * L), 0)
    ts = lax.broadcasted_iota(jnp.int32, (n_sub * L, n_sub * L), 1)
    tril_ones = jnp.where((tt >= ts) & (tt // L == ts // L), 1.0, 0.0).astype(BF16)
    own_lanes = (lax.broadcasted_iota(jnp.int32, (SR, PW), 0) // L
                 == lax.broadcasted_iota(jnp.int32, (SR, PW), 1) // N)
    ti = lax.broadcasted_iota(jnp.int32, (SR, SR), 0)
    si = lax.broadcasted_iota(jnp.int32, (SR, SR), 1)
    strict = ti > si
    incl = ti >= si
    eye_sr = jnp.where(ti == si, 1.0, 0.0)
    eye_pw = jnp.where(lax.broadcasted_iota(jnp.int32, (PW, PW), 0)
                       == lax.broadcasted_iota(jnp.int32, (PW, PW), 1), 1.0, 0.0)

    def stack(x):
        return jnp.where(own_lanes, jnp.concatenate([x, x], axis=0), 0.0)

    ld = ld_ref[...]
    h1 = ld.astype(BF16)
    r1 = ld - h1.astype(F32)
    h2 = r1.astype(BF16)
    h3 = (r1 - h2.astype(F32)).astype(BF16)
    cum = _dot(tril_ones, h1) + (_dot(tril_ones, h2) + _dot(tril_ones, h3))
    w_in = jnp.exp(cum)
    w_inv = jnp.exp(-cum)
    rt = r_ref[...] * w_in
    kt = k_ref[...] * w_inv
    at = -kn_ref[...] * jnp.exp(cum - ld)
    bt = ba_ref[...] * w_inv

    n_pair = RWKV_WIDTH // PW
    chains = [(c, pr) for c in range(n_sub) for pr in range(n_pair)]
    each = lambda fn, *lists: [fn(*xs) for xs in zip(*lists)]
    pieces = lambda z: [stack(z[c * L:(c + 1) * L, pr * PW:(pr + 1) * PW]) for c, pr in chains]
    ast, rst, bst, kst, vst = pieces(at), pieces(rt), pieces(bt), pieces(kt), pieces(v_ref[...])
    bk = each(lambda b, k: jnp.concatenate([b, k], axis=0), bst, kst)
    gram = each(lambda a, r, x: _dot_nt(jnp.concatenate([a, r], axis=0), x), ast, rst, bk)
    a_ab = [jnp.where(strict, x[0:SR, 0:SR], 0.0) for x in gram]
    a_ak = [jnp.where(strict, x[0:SR, SR:2 * SR], 0.0) for x in gram]
    m_rb = [jnp.where(incl, x[SR:2 * SR, 0:SR], 0.0) for x in gram]
    m_rk = [jnp.where(incl, x[SR:2 * SR, SR:2 * SR], 0.0) for x in gram]
    t_inv = [eye_sr + x for x in a_ab]
    power = a_ab
    for _ in range(int(np.log2(L)) - 1):
        power = each(_dot, power, power)
        t_inv = each(lambda t, p: t + _dot(t, p), t_inv, power)
    akv = each(_dot, a_ak, vst)
    au = each(lambda t, a, x: _dot(t, jnp.concatenate([a, x], axis=1)), t_inv, ast, akv)
    ahat = [x[:, 0:PW] for x in au]
    uhat = [x[:, PW:2 * PW] for x in au]
    rhat = each(lambda r, m, a: r + _dot(m, a), rst, m_rb, ahat)
    y0 = each(lambda mb, u, mk, x: _dot(mb, u) + _dot(mk, x), m_rb, uhat, m_rk, vst)
    w_last = [w_in[(c + 1) * L - 1:(c + 1) * L, pr * PW:(pr + 1) * PW] for c, pr in chains]
    p_mat = each(lambda a, b, w: (eye_pw + _dot_tn(a, b)) * w, ahat, bst, w_last)
    q_mat = each(lambda u, x, y, w: _dot_tn(jnp.concatenate([u, x], axis=0), y) * w, uhat, vst, bk, w_last)

    y_chunks = []
    for c in range(n_sub):
        idx = range(c * n_pair, (c + 1) * n_pair)
        s_old = [state[pr] for pr in range(n_pair)]
        y_st = [_dot_nt(rhat[i], s) + y0[i] for i, s in zip(idx, s_old)]
        for pr, i in enumerate(idx):
            state[pr] = _dot(s_old[pr], p_mat[i]) + q_mat[i]
        y_chunks.append(jnp.concatenate([x[0:L] + x[L:SR] for x in y_st], axis=1))
    y = jnp.concatenate(y_chunks, axis=0)
    inv_n = 1.0 / N
    mean = _head_sum(y, bd_ref) * inv_n
    yc = y - mean
    var = _head_sum(yc * yc, bd_ref) * inv_n
    yn = yc * lax.rsqrt(var + LNX_EPS) * lg_ref[...] + lb_ref[...]
    r, k, v = r_ref[...], k_ref[...], v_ref[...]
    bonus = _head_sum(r * k * rk_ref[...], bd_ref) * v
    o_ref[...] = ((yn + bonus) * g_ref[...]).astype(o_ref.dtype)


def _rwkv_scan(r, ld, k, v, kn, ba, g, r_k, lnx_g, lnx_b):
    B, S, W = r.shape
    L = min(RWKV_STEP, S)
    tok = pl.BlockSpec((None, L, W), lambda b, i: (b, i, 0))
    row = lambda z: z.reshape(1, -1)
    return pl.pallas_call(
        _rwkv_scan_kernel,
        grid=(B, S // L),
        in_specs=[tok] * 7 + [_const_spec((1, W))] * 3 + [_const_spec((W, W))],
        out_specs=tok,
        out_shape=jax.ShapeDtypeStruct((B, S, W), BF16),
        scratch_shapes=[pltpu.VMEM((W // (2 * RWKV_HEAD_DIM), 2 * RWKV_HEAD_DIM, 2 * RWKV_HEAD_DIM), F32)],
        compiler_params=_cparams("parallel", "arbitrary"),
        name="rwkv_scan",
    )(r, ld, k, v, kn, ba, g, row(r_k), row(lnx_g), row(lnx_b), _head_block_diag())


def _rwkv_mixer(p, mu, w0, w2, a0, a2, g2, k_k, k_a, r_k, lnx_g, lnx_b, v_first, v_res):
    r, ld, k, v, kn, ba, g = _rwkv_prep(p, mu, w0, w2, a0, a2, g2, k_k, k_a, v_first, v_res)
    y = _rwkv_scan(r, ld, k, v, kn, ba, g, r_k, lnx_g, lnx_b)
    return y, (v if v_res is None else v_first)


def _merge_kernel(x_ref, mod_ref, pg_ref, qg_ref, ya_ref, yb_ref, yc_ref, wg_ref, bw_ref, ow_ref, o_ref):
    x = x_ref[...]
    D = x.shape[1]
    h = _pre(x, pg_ref[...], mod_ref).astype(BF16)
    merged = None
    for i, y_ref in enumerate((ya_ref, yb_ref, yc_ref)):
        gate = _sigmoid(jnp.dot(h, wg_ref[:, i * D:(i + 1) * D], preferred_element_type=F32))
        term = gate * jnp.dot(y_ref[...], bw_ref[i], preferred_element_type=F32)
        merged = term if merged is None else merged + term
    y = _dot(merged, ow_ref[...])
    o_ref[...] = x + mod_ref[2:3, :] * _rms(y, qg_ref[...])


def _merge(x, mod3, pre_g, post_g, y_a, y_b, y_c, w_gate, branch_w, out_w):
    B, S, D = x.shape
    tm = min(FFN_TILE, S)
    tok = lambda n: pl.BlockSpec((None, tm, n), lambda b, i: (b, i, 0))
    return pl.pallas_call(
        _merge_kernel,
        grid=(B, S // tm),
        in_specs=[tok(D), pl.BlockSpec((None, 3, D), lambda b, i: (b, 0, 0)),
                  _const_spec((1, D)), _const_spec((1, D)),
                  tok(BRANCH_WIDTH), tok(BRANCH_WIDTH), tok(BRANCH_WIDTH),
                  _const_spec((D, N_BRANCH * D)), _const_spec((N_BRANCH, BRANCH_WIDTH, D)),
                  _const_spec((D, D))],
        out_specs=tok(D),
        out_shape=jax.ShapeDtypeStruct((B, S, D), F32),
        compiler_params=_cparams("parallel", "parallel"),
        name="merge",
    )(x, mod3, pre_g.reshape(1, D), post_g.reshape(1, D), y_a, y_b, y_c, w_gate, branch_w, out_w)


def kernel(x, c, ada_w, ada_b, pre_g, post_g, ffn_w_in, ffn_w_out, mix_w_in, branch_w, out_w,
           cmp_k_w1, cmp_k_w2, cmp_k_pe, cmp_v_w1, cmp_v_w2, cmp_v_pe,
           sgu_ln_g, sgu_ln_b, sgu_w, sgu_b,
           rwkv_mu, rwkv_w0, rwkv_w2, rwkv_a0, rwkv_a2, rwkv_g2, rwkv_kk, rwkv_ka, rwkv_rk,
           rwkv_lnx_g, rwkv_lnx_b, rwkv_v0, rwkv_v1, rwkv_v2):
    B, S, D = x.shape
    depth = ada_w.shape[0]
    mod = _ada_mod(c, ada_w, ada_b).reshape(depth, B, 3, 3, D)
    v_first = None
    for l in range(depth):
        x = _ffn(x, mod[l, :, 0], pre_g[l, 0], post_g[l, 0],
                 ffn_w_in[l, 0].astype(BF16), ffn_w_out[l, 0].astype(BF16))

        w = mix_w_in[l]
        n_in = NSA_COLS + 2 * SGU_WIDTH + RWKV_COLS
        w_proj = jnp.concatenate(
            [w[:, :Q_COLS + KV_COLS],
             jnp.pad(w[:, Q_COLS + KV_COLS:NSA_COLS], ((0, 0), (0, LANES - GT_COLS))),
             w[:, NSA_COLS:n_in]], axis=1).astype(BF16)
        q, kv, gt, p_sgu, p_rwkv = _mix_proj(x, mod[l, :, 1], pre_g[l, 1], w_proj)
        y_a = _nsa_mixer(q, kv, gt, cmp_k_w1[l], cmp_k_w2[l], cmp_k_pe[l],
                         cmp_v_w1[l], cmp_v_w2[l], cmp_v_pe[l])
        y_b = _sgu_mixer(p_sgu, sgu_ln_g[l], sgu_ln_b[l], sgu_w[l], sgu_b[l])
        v_res = None if l == 0 else (rwkv_v0[l - 1], rwkv_v1[l - 1], rwkv_v2[l - 1])
        y_c, v_first = _rwkv_mixer(p_rwkv, rwkv_mu[l], rwkv_w0[l], rwkv_w2[l], rwkv_a0[l], rwkv_a2[l],
                                   rwkv_g2[l], rwkv_kk[l], rwkv_ka[l], rwkv_rk[l],
                                   rwkv_lnx_g[l], rwkv_lnx_b[l], v_first, v_res)
        x = _merge(x, mod[l, :, 1], pre_g[l, 1], post_g[l, 1], y_a, y_b, y_c,
                   w[:, n_in:].astype(BF16), branch_w[l].astype(BF16), out_w[l].astype(BF16))

        x = _ffn(x, mod[l, :, 2], pre_g[l, 2], post_g[l, 2],
                 ffn_w_in[l, 1].astype(BF16), ffn_w_out[l, 1].astype(BF16))
    return x
```

```python
import functools

import numpy as np
import jax
import jax.numpy as jnp
from jax import lax
from jax.experimental import pallas as pl
from jax.experimental.pallas import tpu as pltpu

D_MODEL = 1024
DEPTH = 2
NSA_HEADS = 8
NSA_GROUPS = 2
NSA_HPG = NSA_HEADS // NSA_GROUPS
NSA_DK = 64
CMP_LEN = 32
CMP_STRIDE = 16
CMP_HID = 256
SLC_BLOCK = 64
SLC_TOPN = 16
WINDOW = 512
Q_BLOCK = 128
SGU_CHUNK = 128
SGU_GROUPS = 4
SGU_WIDTH = 512
RWKV_HEADS = 8
RWKV_HEAD_DIM = 64
RWKV_WIDTH = RWKV_HEADS * RWKV_HEAD_DIM
DECAY_LORA = 64
AAA_LORA = 64
MV_LORA = 32
GATE_LORA = 128
N_BRANCH = 3
BRANCH_WIDTH = 512
D_FF = 2816
MACARON_WEIGHT = 0.5
RMS_EPS = 1e-6
LN_EPS = 1e-5
LNX_EPS = 64e-5
NEG_INF = -1e30
FORCE_SCORE = 1e4

NSA_COLS = NSA_HEADS * NSA_DK + 6 * NSA_GROUPS * NSA_DK + 3 * NSA_HEADS
RWKV_COLS = 3 * RWKV_WIDTH + DECAY_LORA + AAA_LORA + GATE_LORA
Q_COLS = NSA_HEADS * NSA_DK
KV_COLS = 6 * NSA_GROUPS * NSA_DK
GT_COLS = 3 * NSA_HEADS
LANES = 128
NSA_STEP = 256
NSA_VAUG = 128
NSA_AUG = 128
POS_SPLIT = 64
MASK_BIG = 2.0 ** 100

BF16 = jnp.bfloat16
F32 = jnp.float32

FFN_TILE = 512
FFN_CHUNK = 1408
SEL_TILE = 512
RWKV_CHUNK = 64
RWKV_STEP = 128
RWKV_PREP_TILE = 256
SGU_TILE = 512
VMEM_LIMIT = 56 * 1024 * 1024


def _cparams(*sem):
    return pltpu.CompilerParams(dimension_semantics=sem, vmem_limit_bytes=VMEM_LIMIT)


def _dot(a, b):
    return jnp.dot(a.astype(BF16), b.astype(BF16), preferred_element_type=F32)


def _dot_nt(a, b):
    return lax.dot_general(a.astype(BF16), b.astype(BF16), (((1,), (1,)), ((), ())),
                           preferred_element_type=F32)


def _dot_tn(a, b):
    return lax.dot_general(a.astype(BF16), b.astype(BF16), (((0,), (0,)), ((), ())),
                           preferred_element_type=F32)


def _split(x):
    hi = x.astype(BF16)
    lo = (x - hi.astype(F32)).astype(BF16)
    return hi, lo


def _dot_x2(a, b):
    hi, lo = _split(a)
    return _dot(hi, b) + _dot(lo, b)


def _sigmoid(x):
    return 1.0 / (1.0 + jnp.exp(-x))


def _silu(x):
    return x * _sigmoid(x)


def _rms(x, g):
    return x * lax.rsqrt(jnp.mean(x * x, axis=-1, keepdims=True) + RMS_EPS) * g


def _pre(x, g, mod_ref):
    return _rms(x, g) * (1.0 + mod_ref[1:2, :]) + mod_ref[0:1, :]


def _ada_kernel(c_ref, w_ref, b_ref, o_ref):
    cond = _silu(c_ref[...])
    o_ref[...] = _dot(cond, w_ref[...]) + b_ref[...]


def _ada_mod(c, ada_w, ada_b):
    L, D, N = ada_w.shape
    B = c.shape[0]
    tn = 1536
    return pl.pallas_call(
        _ada_kernel,
        grid=(L, N // tn),
        in_specs=[
            pl.BlockSpec((B, D), lambda l, j: (0, 0)),
            pl.BlockSpec((None, D, tn), lambda l, j: (l, 0, j)),
            pl.BlockSpec((None, 1, tn), lambda l, j: (l, 0, j)),
        ],
        out_specs=pl.BlockSpec((None, B, tn), lambda l, j: (l, 0, j)),
        out_shape=jax.ShapeDtypeStruct((L, B, N), F32),
        compiler_params=_cparams("arbitrary", "arbitrary"),
        name="ada_mod",
    )(c, ada_w, ada_b.reshape(L, 1, N))


def _ffn_kernel(x_ref, mod_ref, pg_ref, qg_ref, win_ref, wout_ref, o_ref):
    x = x_ref[...]
    h = _pre(x, pg_ref[...], mod_ref).astype(BF16)
    acc = None
    for c in range(D_FF // FFN_CHUNK):
        lo, hi = c * FFN_CHUNK, (c + 1) * FFN_CHUNK
        gate = jnp.dot(h, win_ref[:, lo:hi], preferred_element_type=F32)
        up = jnp.dot(h, win_ref[:, D_FF + lo:D_FF + hi], preferred_element_type=F32)
        act = (_silu(gate) * up).astype(BF16)
        part = jnp.dot(act, wout_ref[lo:hi, :], preferred_element_type=F32)
        acc = part if acc is None else acc + part
    o_ref[...] = x + MACARON_WEIGHT * mod_ref[2:3, :] * _rms(acc, qg_ref[...])


def _const_spec(shape):
    nd = len(shape)
    return pl.BlockSpec(shape, lambda *_: (0,) * nd, pipeline_mode=pl.Buffered(1))


def _ffn(x, mod3, pre_g, post_g, w_in, w_out):
    B, S, D = x.shape
    tm = min(FFN_TILE, S)
    return pl.pallas_call(
        _ffn_kernel,
        grid=(B, S // tm),
        in_specs=[
            pl.BlockSpec((None, tm, D), lambda b, i: (b, i, 0)),
            pl.BlockSpec((None, 3, D), lambda b, i: (b, 0, 0)),
            _const_spec((1, D)),
            _const_spec((1, D)),
            _const_spec((D, 2 * D_FF)),
            _const_spec((D_FF, D)),
        ],
        out_specs=pl.BlockSpec((None, tm, D), lambda b, i: (b, i, 0)),
        out_shape=jax.ShapeDtypeStruct((B, S, D), F32),
        compiler_params=_cparams("parallel", "parallel"),
        name="ffn",
    )(x, mod3, pre_g.reshape(1, D), post_g.reshape(1, D), w_in, w_out)


Q_AUG_COLS = NSA_HEADS * NSA_AUG
V_AUG_COLS = NSA_GROUPS * NSA_VAUG
CMP_COLS = 2 * NSA_GROUPS * NSA_DK
PROJ_WIDTHS = (Q_AUG_COLS, CMP_COLS, V_AUG_COLS, V_AUG_COLS, LANES, 2 * SGU_WIDTH, RWKV_COLS)
PROJ_COLS = sum(PROJ_WIDTHS)
KT_ROWS = 2 * NSA_GROUPS * NSA_AUG


def _proj_kernel(x_ref, mod_ref, pg_ref, w_ref, wkt_ref, qc_ref, vc_ref,
                 q_ref, cmp_ref, vs_ref, vw_ref, gt_ref, sgu_ref, rw_ref, ks_ref, kw_ref):
    h = _pre(x_ref[...], pg_ref[...], mod_ref).astype(BF16)
    tm = h.shape[0]
    edges = np.cumsum((0,) + PROJ_WIDTHS)
    cols = [jnp.dot(h, w_ref[:, int(a):int(b)], preferred_element_type=F32)
            for a, b in zip(edges[:-1], edges[1:])]
    q_ref[...] = (cols[0] * NSA_DK ** -0.5 + qc_ref[...]).astype(BF16)
    cmp_ref[...] = cols[1].astype(BF16)
    vs_ref[...] = (cols[2] + vc_ref[...]).astype(BF16)
    vw_ref[...] = (cols[3] + vc_ref[...]).astype(BF16)
    gt_ref[...] = cols[4]
    sgu_ref[...] = cols[5]
    rw_ref[...] = cols[6]
    kt = _dot_nt(wkt_ref[...], h)
    pos = pl.program_id(1) * tm + lax.broadcasted_iota(jnp.int32, (1, tm), 1)
    row = lax.broadcasted_iota(jnp.int32, (NSA_AUG, 1), 0)
    pos_rows = jnp.where(row == NSA_DK, (pos // POS_SPLIT).astype(F32),
                         jnp.where(row == NSA_DK + 1, (pos % POS_SPLIT).astype(F32), 0.0))
    for g in range(NSA_GROUPS):
        ks_ref[g, 0] = (kt[g * NSA_AUG:(g + 1) * NSA_AUG] + pos_rows).astype(BF16)
        kw = (kt[(NSA_GROUPS + g) * NSA_AUG:(NSA_GROUPS + g + 1) * NSA_AUG] + pos_rows).astype(BF16)
        for t in range(tm // LANES):
            kw_ref[g, t] = kw[:, t * LANES:(t + 1) * LANES]


def _proj_weights(w):
    D = w.shape[0]
    G, DK = NSA_GROUPS, NSA_DK
    q, kc, vc, ks, vs, kw, vw = [w[:, a:b] for a, b in zip(
        np.cumsum((0, Q_COLS) + (G * DK,) * 5), np.cumsum((Q_COLS,) + (G * DK,) * 6))]
    gt = w[:, Q_COLS + KV_COLS:NSA_COLS]

    def padded(z, width):
        n = z.shape[1] // DK
        return jnp.pad(z.reshape(D, n, DK), ((0, 0), (0, 0), (0, width - DK))).reshape(D, n * width)

    token_major = jnp.concatenate(
        [padded(q, NSA_AUG), kc, vc, padded(vs, NSA_VAUG), padded(vw, NSA_VAUG),
         jnp.pad(gt, ((0, 0), (0, LANES - GT_COLS))), w[:, NSA_COLS:NSA_COLS + 2 * SGU_WIDTH + RWKV_COLS]],
        axis=1).astype(BF16)
    keys_t = jnp.concatenate([padded(ks, NSA_AUG), padded(kw, NSA_AUG)], axis=1).T.astype(BF16)
    return token_major, keys_t


def _mix_proj(x, mod3, pre_g, w, wkt):
    B, S, D = x.shape
    tm = min(FFN_TILE, S)
    slopes = 2.0 ** -np.arange(1, NSA_HEADS + 1, dtype=np.float32)
    q_const = np.zeros((NSA_HEADS, NSA_AUG), np.float32)
    q_const[:, NSA_DK] = POS_SPLIT * slopes
    q_const[:, NSA_DK + 1] = slopes
    v_const = np.zeros((NSA_GROUPS, NSA_VAUG), np.float32)
    v_const[:, NSA_DK] = 1.0
    dtypes = (BF16, BF16, BF16, BF16, F32, F32, F32)
    tok = lambda n: pl.BlockSpec((None, tm, n), lambda b, i: (b, i, 0))
    return pl.pallas_call(
        _proj_kernel,
        grid=(B, S // tm),
        in_specs=[
            tok(D),
            pl.BlockSpec((None, 3, D), lambda b, i: (b, 0, 0)),
            _const_spec((1, D)),
            _const_spec((D, PROJ_COLS)),
            _const_spec((KT_ROWS, D)),
            _const_spec((1, Q_AUG_COLS)),
            _const_spec((1, V_AUG_COLS)),
        ],
        out_specs=[tok(n) for n in PROJ_WIDTHS] + [
            pl.BlockSpec((None, NSA_GROUPS, 1, NSA_AUG, tm), lambda b, i: (b, 0, i, 0, 0)),
            pl.BlockSpec((None, NSA_GROUPS, tm // LANES, NSA_AUG, LANES), lambda b, i: (b, 0, i, 0, 0))],
        out_shape=[jax.ShapeDtypeStruct((B, S, n), dt) for n, dt in zip(PROJ_WIDTHS, dtypes)] + [
            jax.ShapeDtypeStruct((B, NSA_GROUPS, S // tm, NSA_AUG, tm), BF16),
            jax.ShapeDtypeStruct((B, NSA_GROUPS, S // LANES, NSA_AUG, LANES), BF16)],
        compiler_params=_cparams("parallel", "parallel"),
        name="mix_proj",
    )(x, mod3, pre_g.reshape(1, D), w, wkt,
      jnp.asarray(q_const.reshape(1, -1)), jnp.asarray(v_const.reshape(1, -1)))


def _cmp_kernel(xk_ref, xv_ref, pek_ref, pev_ref, w1k_ref, w1v_ref, w2kt_ref, w2v_ref,
                kct_ref, vc_ref):
    half = (CMP_LEN // 2) * NSA_DK

    def hidden(x_ref, pe_ref, w1_ref):
        x = x_ref[...].astype(F32)
        n = x.shape[0]
        top = _dot(x + pe_ref[0:1, :], w1_ref[0:half, :])
        bot = _dot(x + pe_ref[1:2, :], w1_ref[half:2 * half, :])
        return _silu(top + pltpu.roll(bot, n - 1, axis=0))

    kct_ref[...] = _dot_nt(w2kt_ref[...], hidden(xk_ref, pek_ref, w1k_ref)).astype(BF16)
    vc_ref[...] = _dot(hidden(xv_ref, pev_ref, w1v_ref), w2v_ref[...]).astype(BF16)


def _compress(xk, xv, pe_k, pe_v, w1k, w1v, w2kt, w2v):
    B, G, NC, F = xk.shape
    blk = lambda *s: pl.BlockSpec((None, None) + s, lambda b, g: (b, g, 0, 0))
    return pl.pallas_call(
        _cmp_kernel,
        grid=(B, G),
        in_specs=[blk(NC, F), blk(NC, F), _const_spec((2, F)), _const_spec((2, F)),
                  _const_spec((2 * F, CMP_HID)), _const_spec((2 * F, CMP_HID)),
                  _const_spec((NSA_DK, CMP_HID)), _const_spec((CMP_HID, NSA_VAUG))],
        out_specs=[blk(NSA_DK, NC), blk(NC, NSA_VAUG)],
        out_shape=[jax.ShapeDtypeStruct((B, G, NSA_DK, NC), BF16),
                   jax.ShapeDtypeStruct((B, G, NC, NSA_VAUG), BF16)],
        compiler_params=_cparams("parallel", "parallel"),
        name="nsa_compress",
    )(xk, xv, pe_k, pe_v, w1k, w1v, w2kt, w2v)


def _masked_softmax(s, valid):
    s = jnp.where(valid, s, NEG_INF)
    m = jnp.max(s, axis=-1, keepdims=True)
    p = jnp.where(valid, jnp.exp(s - m), 0.0)
    l = jnp.sum(p, axis=-1, keepdims=True)
    return p * (1.0 / jnp.where(l > 0.0, l, 1.0))


def _nsa_kernel(q_ref, gt_ref, kc_ref, vc_ref, ks_ref, vs_ref, kw_ref, vw_ref, selt_ref, e_ref,
                o_ref, imp_ref, *, n_top, tk):
    s0 = pl.program_id(2) * NSA_STEP
    R = NSA_HPG * Q_BLOCK
    n_slc = selt_ref.shape[0]
    n_cmp = kc_ref.shape[-1]
    parts = range(NSA_STEP // Q_BLOCK)
    each = lambda fn, *lists: [fn(*xs) for xs in zip(*lists)]
    rows = lambda c: slice(c * Q_BLOCK, (c + 1) * Q_BLOCK)
    head_cols = lambda h: slice(h * NSA_AUG, (h + 1) * NSA_AUG)
    q = [jnp.concatenate([q_ref[rows(c), head_cols(h)] for h in range(NSA_HPG)], axis=0) for c in parts]
    q_row = lax.broadcasted_iota(jnp.int32, (R, 1), 0) & (Q_BLOCK - 1)
    t_i = [s0 + c * Q_BLOCK + q_row for c in parts]

    def normalised(acc):
        return acc * (1.0 / acc[:, NSA_DK:NSA_DK + 1])

    cmp_end = lax.broadcasted_iota(jnp.int32, (1, n_cmp), 1) * CMP_STRIDE + (CMP_LEN - 1)
    s_c = [jnp.dot(x, kc_ref[...], preferred_element_type=F32) for x in q]
    p_c = each(lambda s, t: _masked_softmax(s, cmp_end <= t), s_c, t_i)
    o_c = [_dot(p, vc_ref[...]) for p in p_c]

    j = lax.broadcasted_iota(jnp.int32, (n_slc, Q_BLOCK), 0)
    imp = []
    for c in parts:
        p_sum = p_c[c][0:Q_BLOCK]
        for h in range(1, NSA_HPG):
            p_sum = p_sum + p_c[c][h * Q_BLOCK:(h + 1) * Q_BLOCK]
        hi, lo = _split(p_sum)
        raw = _dot_nt(selt_ref[...], hi) + _dot_nt(selt_ref[...], lo)
        tq = s0 + c * Q_BLOCK + lax.broadcasted_iota(jnp.int32, (n_slc, Q_BLOCK), 1)
        cur = tq // SLC_BLOCK
        forced = (j == 0) | (j == cur) | (j == cur - 1)
        live = j * SLC_BLOCK <= tq
        imp.append(jnp.where(forced, FORCE_SCORE, jnp.where(live, raw, NEG_INF)))
        imp_ref[c] = imp[c]

    def rank_body(i, cnt):
        tie = jnp.where(j > i, 1.0, 0.0)
        out = []
        for c in parts:
            r = imp_ref[c, pl.ds(i, 1), :]
            out.append(cnt[c] + jnp.where(r > imp[c], 1.0, jnp.where(r == imp[c], tie, 0.0)))
        return tuple(out)

    n_live = (s0 + NSA_STEP) // SLC_BLOCK
    cnt = lax.fori_loop(0, n_live, rank_body, tuple(jnp.zeros((n_slc, Q_BLOCK), F32) for _ in parts))
    lhs = []
    for c in parts:
        dropped = jnp.concatenate(
            [jnp.where(cnt[c] < n_top, 0.0, 1.0), jnp.zeros((NSA_AUG - n_slc, Q_BLOCK), F32)], axis=0)
        dropped = dropped.T.astype(BF16)
        lhs.append(jnp.concatenate([q[c], jnp.concatenate([dropped] * NSA_HPG, axis=0)], axis=1))

    def sel_tile(kt, carry, diagonal):
        m, acc = carry[0::2], carry[1::2]
        rhs = jnp.concatenate([ks_ref[kt], e_ref[kt]], axis=0)
        v = vs_ref[pl.ds(pl.multiple_of(kt * tk, tk), tk), :]
        s = [jnp.dot(x, rhs, preferred_element_type=F32) for x in lhs]
        if diagonal:
            pos = kt * tk + lax.broadcasted_iota(jnp.int32, (1, tk), 1)
            s = each(lambda x, t: jnp.where(pos <= t, x, NEG_INF), s, t_i)
        m_new = each(lambda x, y: jnp.maximum(x, jnp.max(y, axis=-1, keepdims=True)), m, s)
        alpha = each(lambda x, y: jnp.exp(x - y), m, m_new)
        p = each(lambda x, y: jnp.exp(x - y).astype(BF16), s, m_new)
        acc = each(lambda a, x, y: a * x + jnp.dot(y, v, preferred_element_type=F32), alpha, acc, p)
        out = []
        for x, y in zip(m_new, acc):
            out += [x, y]
        return tuple(out)

    last = (s0 + NSA_STEP - 1) // tk
    init = []
    for _ in parts:
        init += [jnp.full((R, 1), NEG_INF, F32), jnp.zeros((R, NSA_VAUG), F32)]
    carry = lax.fori_loop(0, last, functools.partial(sel_tile, diagonal=False), tuple(init))
    o_s = [normalised(x) for x in sel_tile(last, carry, True)[1::2]]

    n_wt = (WINDOW + Q_BLOCK) // LANES
    lane_pos = lax.broadcasted_iota(jnp.int32, (1, LANES), 1)
    s_w, start = [], []
    for c in parts:
        start.append(jnp.maximum(s0 + c * Q_BLOCK - WINDOW, 0))
        tiles = []
        for i in range(n_wt):
            pos = start[c] + i * LANES + lane_pos
            x = jnp.dot(q[c], kw_ref[start[c] // LANES + i], preferred_element_type=F32)
            x = jnp.where(pos <= t_i[c], x, NEG_INF)
            if i == 0:
                x = jnp.where(pos > t_i[c] - WINDOW, x, NEG_INF)
            tiles.append(x)
        s_w.append(jnp.concatenate(tiles, axis=1))
    p_w = [jnp.exp(x - jnp.max(x, axis=-1, keepdims=True)) for x in s_w]
    o_w = [normalised(_dot(p, vw_ref[pl.ds(pl.multiple_of(st, LANES), n_wt * LANES), :]))
           for p, st in zip(p_w, start)]

    for c in parts:
        gate = _sigmoid(gt_ref[:, rows(c), :].reshape(R, 3))
        o = gate[:, 0:1] * o_c[c] + gate[:, 1:2] * o_s[c] + gate[:, 2:3] * o_w[c]
        for h in range(NSA_HPG):
            o_ref[rows(c), head_cols(h)] = o[h * Q_BLOCK:(h + 1) * Q_BLOCK].astype(o_ref.dtype)


def _selection_map_t(n_cmp_rows, n_slc):
    rs, rc = SLC_BLOCK // CMP_STRIDE, CMP_LEN // CMP_STRIDE
    m = np.zeros((n_slc, n_cmp_rows), np.float32)
    for jj in range(n_slc):
        for a in range(rs):
            for b in range(rc):
                i = rs * jj - a - b
                if 0 <= i < n_cmp_rows - 1:
                    m[jj, i] += 1.0
    return jnp.asarray(m, BF16)


def _nsa_attention(q, gt, kc, vc, ks, vs, kw, vw):
    B, S, _ = q.shape
    G, A = NSA_GROUPS, NSA_AUG
    n_slc = S // SLC_BLOCK
    n_cmp = kc.shape[-1]
    n_kt, _, tk = ks.shape[2:]
    selt = _selection_map_t(n_cmp, n_slc)
    key_blk = np.arange(S).reshape(n_kt, 1, tk) // SLC_BLOCK
    e = np.where(np.arange(A).reshape(1, A, 1) == key_blk, -MASK_BIG, 0.0)
    heads = pl.BlockSpec((None, NSA_STEP, NSA_HPG * A), lambda b, g, i: (b, i, g))
    values = pl.BlockSpec((None, S, NSA_VAUG), lambda b, g, i: (b, 0, g))
    per_g = lambda *s: pl.BlockSpec((None, None) + s, lambda b, g, i: (b, g) + (0,) * len(s))
    kern = functools.partial(_nsa_kernel, n_top=min(SLC_TOPN, n_slc), tk=tk)
    return pl.pallas_call(
        kern,
        grid=(B, G, S // NSA_STEP),
        in_specs=[heads, pl.BlockSpec((None, NSA_HPG, NSA_STEP, 3), lambda b, g, i: (b, g, i, 0)),
                  per_g(A, n_cmp), per_g(n_cmp, NSA_VAUG),
                  per_g(n_kt, A, tk), values, per_g(S // LANES, A, LANES), values,
                  pl.BlockSpec((n_slc, n_cmp), lambda b, g, i: (0, 0)),
                  pl.BlockSpec((n_kt, A, tk), lambda b, g, i: (0, 0, 0))],
        out_specs=heads,
        out_shape=jax.ShapeDtypeStruct((B, S, NSA_HEADS * NSA_VAUG), BF16),
        scratch_shapes=[pltpu.VMEM((NSA_STEP // Q_BLOCK, n_slc, Q_BLOCK), F32)],
        compiler_params=_cparams("parallel", "parallel", "arbitrary"),
        name="nsa_attention",
    )(q, gt, kc, vc, ks, vs, kw, vw, selt, jnp.asarray(e, BF16))


def _with_position_rows(kt, pos):
    lead = kt.shape[:-2]
    n = kt.shape[-1]
    hi = jnp.broadcast_to(jnp.asarray(pos // POS_SPLIT, BF16)[..., None, :], lead + (1, n))
    lo = jnp.broadcast_to(jnp.asarray(pos % POS_SPLIT, BF16)[..., None, :], lead + (1, n))
    pad = jnp.zeros(lead + (NSA_AUG - NSA_DK - 2, n), BF16)
    return jnp.concatenate([kt, hi, lo, pad], axis=-2)


def _nsa_mixer(q, kvc, vs, vw, gt, ks, kw, k_w1, k_w2, k_pe, v_w1, v_w2, v_pe):
    B, S, _ = q.shape
    G, DK = NSA_GROUPS, NSA_DK
    half = CMP_LEN // 2
    nc = S // half

    def half_blocks(z):
        z = z.reshape(B, nc, half, G, DK)
        return z.transpose(0, 3, 1, 2, 4).reshape(B, G, nc, half * DK)

    kct, vcc = _compress(
        half_blocks(kvc[:, :, :G * DK]), half_blocks(kvc[:, :, G * DK:]),
        k_pe.reshape(2, half * DK), v_pe.reshape(2, half * DK),
        k_w1.astype(BF16), v_w1.astype(BF16), k_w2.T.astype(BF16),
        jnp.pad(v_w2, ((0, 0), (0, NSA_VAUG - DK))).astype(BF16))
    gth = gt[:, :, :GT_COLS].reshape(B, S, NSA_HEADS, 3).transpose(0, 2, 1, 3)
    kca = _with_position_rows(kct, np.arange(nc) * CMP_STRIDE + (CMP_LEN - 1))
    return _nsa_attention(q, gth, kca, vcc, ks, vs, kw, vw)


def _gelu_tanh(x):
    return 0.5 * x * (1.0 + jnp.tanh(np.sqrt(2.0 / np.pi).astype(np.float32) * (x + 0.044715 * (x * x * x))))


def _sgu_kernel(p_ref, g_ref, b_ref, w_ref, bs_ref, o_ref):
    ts = p_ref.shape[0]
    gd = SGU_WIDTH // SGU_GROUPS
    u = _gelu_tanh(p_ref[:, 0:SGU_WIDTH])
    v = _gelu_tanh(p_ref[:, SGU_WIDTH:2 * SGU_WIDTH])
    mu = jnp.mean(v, axis=-1, keepdims=True)
    vc = v - mu
    var = jnp.mean(vc * vc, axis=-1, keepdims=True)
    vn = (vc * lax.rsqrt(var + LN_EPS) * g_ref[...] + b_ref[...]).astype(BF16)
    causal = (lax.broadcasted_iota(jnp.int32, (SGU_CHUNK, SGU_CHUNK), 0)
              >= lax.broadcasted_iota(jnp.int32, (SGU_CHUNK, SGU_CHUNK), 1))
    ws = [jnp.where(causal, w_ref[i], 0.0).astype(BF16) for i in range(SGU_GROUPS)]
    for n in range(ts // SGU_CHUNK):
        r0, r1 = n * SGU_CHUNK, (n + 1) * SGU_CHUNK
        s = jnp.concatenate(
            [jnp.dot(ws[i], vn[r0:r1, i * gd:(i + 1) * gd], preferred_element_type=F32)
             for i in range(SGU_GROUPS)], axis=1) + bs_ref[...]
        o_ref[r0:r1, :] = (u[r0:r1] * s).astype(o_ref.dtype)


def _sgu_mixer(p, ln_g, ln_b, w_s, b_s):
    B, S, _ = p.shape
    ts = min(SGU_TILE, S)
    gd = SGU_WIDTH // SGU_GROUPS
    bias = jnp.repeat(b_s.T, gd, axis=1)
    return pl.pallas_call(
        _sgu_kernel,
        grid=(B, S // ts),
        in_specs=[pl.BlockSpec((None, ts, 2 * SGU_WIDTH), lambda b, i: (b, i, 0)),
                  _const_spec((1, SGU_WIDTH)), _const_spec((1, SGU_WIDTH)),
                  _const_spec((SGU_GROUPS, SGU_CHUNK, SGU_CHUNK)), _const_spec((SGU_CHUNK, SGU_WIDTH))],
        out_specs=pl.BlockSpec((None, ts, SGU_WIDTH), lambda b, i: (b, i, 0)),
        out_shape=jax.ShapeDtypeStruct((B, S, SGU_WIDTH), BF16),
        compiler_params=_cparams("parallel", "parallel"),
        name="sgu",
    )(p, ln_g.reshape(1, -1), ln_b.reshape(1, -1), w_s, bias)


def _head_sum(x, bd_ref):
    return _dot_x2(x, bd_ref[...])


def _rwkv_prep_kernel(*refs, has_vres):
    if has_vres:
        (p_ref, mu_ref, w0_ref, w2_ref, a0_ref, a2_ref, g2_ref, kk_ref, ka_ref, bd_ref,
         vf_ref, v0_ref, v1_ref, v2_ref,
         r_o, ld_o, k_o, v_o, kn_o, ba_o, g_o, carry) = refs
    else:
        (p_ref, mu_ref, w0_ref, w2_ref, a0_ref, a2_ref, g2_ref, kk_ref, ka_ref, bd_ref,
         r_o, ld_o, k_o, v_o, kn_o, ba_o, g_o, carry) = refs
    W = RWKV_WIDTH
    p = p_ref[...]
    tr = p.shape[0]

    @pl.when(pl.program_id(1) == 0)
    def _():
        carry[...] = jnp.zeros_like(carry)

    rowi = lax.broadcasted_iota(jnp.int32, (tr, 1), 0)
    p_prev = jnp.where(rowi == 0, carry[0:1, :], pltpu.roll(p, 1, axis=0))
    carry[0:1, :] = p[tr - 1:tr, :]
    ps = p + (p_prev - p) * mu_ref[...]
    r, k, v = ps[:, 0:W], ps[:, W:2 * W], ps[:, 2 * W:3 * W]
    wa = ps[:, 3 * W:3 * W + DECAY_LORA + AAA_LORA]
    gd = ps[:, 3 * W + DECAY_LORA + AAA_LORA:]
    w = w0_ref[...] + _dot(jnp.tanh(wa), w2_ref[...])
    x = -w
    softplus = jnp.maximum(x, 0.0) + jnp.log(1.0 + jnp.exp(-jnp.abs(x)))
    ld_o[...] = -jnp.exp(-softplus - 0.5)
    a = _sigmoid(a0_ref[...] + _dot(wa, a2_ref[...]))
    g_o[...] = _dot(_sigmoid(gd), g2_ref[...])
    if has_vres:
        lora = _dot(_dot(v, v1_ref[...]), v2_ref[...])
        v = v + (vf_ref[...] - v) * _sigmoid(v0_ref[...] + lora)
    kk = k * kk_ref[...]
    norm = jnp.sqrt(_head_sum(kk * kk, bd_ref))
    kn = kk * (1.0 / jnp.maximum(norm, 1e-12))
    r_o[...] = r
    k_o[...] = k * (1.0 + (a - 1.0) * ka_ref[...])
    v_o[...] = v
    kn_o[...] = kn
    ba_o[...] = kn * a


def _head_block_diag():
    idx = np.arange(RWKV_WIDTH) // RWKV_HEAD_DIM
    return jnp.asarray(idx[:, None] == idx[None, :], BF16)


def _rwkv_prep(p, mu, w0, w2, a0, a2, g2, k_k, k_a, v_first, v_res):
    B, S, C = p.shape
    W = RWKV_WIDTH
    tr = min(RWKV_PREP_TILE, S)
    row = lambda z: z.reshape(1, -1)
    lora_rows = DECAY_LORA + AAA_LORA
    w2p = jnp.zeros((lora_rows, W), F32).at[:DECAY_LORA].set(w2).astype(BF16)
    a2p = jnp.zeros((lora_rows, W), F32).at[DECAY_LORA:].set(a2).astype(BF16)
    tok = lambda n: pl.BlockSpec((None, tr, n), lambda b, i: (b, i, 0))
    args = [p, row(mu), row(w0), w2p, row(a0), a2p, g2.astype(BF16), row(k_k), row(k_a), _head_block_diag()]
    specs = [tok(C), _const_spec((1, C)), _const_spec((1, W)), _const_spec((lora_rows, W)),
             _const_spec((1, W)), _const_spec((lora_rows, W)), _const_spec((GATE_LORA, W)),
             _const_spec((1, W)), _const_spec((1, W)), _const_spec((W, W))]
    if v_res is not None:
        v0, v1, v2 = v_res
        args += [v_first, row(v0), v1.astype(BF16), v2.astype(BF16)]
        specs += [tok(W), _const_spec((1, W)), _const_spec((W, MV_LORA)), _const_spec((MV_LORA, W))]
    return pl.pallas_call(
        functools.partial(_rwkv_prep_kernel, has_vres=v_res is not None),
        grid=(B, S // tr),
        in_specs=specs,
        out_specs=[tok(W)] * 7,
        out_shape=[jax.ShapeDtypeStruct((B, S, W), F32)] * 7,
        scratch_shapes=[pltpu.VMEM((8, C), F32)],
        compiler_params=_cparams("parallel", "arbitrary"),
        name="rwkv_prep",
    )(*args)


def _rwkv_scan_kernel(r_ref, ld_ref, k_ref, v_ref, kn_ref, ba_ref, g_ref, rk_ref, lg_ref, lb_ref,
                      bd_ref, o_ref, state):
    L = RWKV_CHUNK
    N = RWKV_HEAD_DIM
    PW = 2 * N
    SR = 2 * L
    n_sub = r_ref.shape[0] // L

    @pl.when(pl.program_id(1) == 0)
    def _():
        state[...] = jnp.zeros_like(state)

    tt = lax.broadcasted_iota(jnp.int32, (n_sub * L, n_sub * L), 0)
    ts = lax.broadcasted_iota(jnp.int32, (n_sub * L, n_sub * L), 1)
    tril_ones = jnp.where((tt >= ts) & (tt // L == ts // L), 1.0, 0.0).astype(BF16)
    own_lanes = (lax.broadcasted_iota(jnp.int32, (SR, PW), 0) // L
                 == lax.broadcasted_iota(jnp.int32, (SR, PW), 1) // N)
    ti = lax.broadcasted_iota(jnp.int32, (SR, SR), 0)
    si = lax.broadcasted_iota(jnp.int32, (SR, SR), 1)
    strict = ti > si
    incl = ti >= si
    eye_sr = jnp.where(ti == si, 1.0, 0.0)
    eye_pw = jnp.where(lax.broadcasted_iota(jnp.int32, (PW, PW), 0)
                       == lax.broadcasted_iota(jnp.int32, (PW, PW), 1), 1.0, 0.0)

    def stack(x):
        return jnp.where(own_lanes, jnp.concatenate([x, x], axis=0), 0.0)

    ld = ld_ref[...]
    h1 = ld.astype(BF16)
    r1 = ld - h1.astype(F32)
    h2 = r1.astype(BF16)
    h3 = (r1 - h2.astype(F32)).astype(BF16)
    cum = _dot(tril_ones, h1) + (_dot(tril_ones, h2) + _dot(tril_ones, h3))
    w_in = jnp.exp(cum)
    w_inv = jnp.exp(-cum)
    rt = r_ref[...] * w_in
    kt = k_ref[...] * w_inv
    at = -kn_ref[...] * jnp.exp(cum - ld)
    bt = ba_ref[...] * w_inv

    n_pair = RWKV_WIDTH // PW
    chains = [(c, pr) for c in range(n_sub) for pr in range(n_pair)]
    each = lambda fn, *lists: [fn(*xs) for xs in zip(*lists)]
    pieces = lambda z: [stack(z[c * L:(c + 1) * L, pr * PW:(pr + 1) * PW]) for c, pr in chains]
    ast, rst, bst, kst, vst = pieces(at), pieces(rt), pieces(bt), pieces(kt), pieces(v_ref[...])
    bk = each(lambda b, k: jnp.concatenate([b, k], axis=0), bst, kst)
    gram = each(lambda a, r, x: _dot_nt(jnp.concatenate([a, r], axis=0), x), ast, rst, bk)
    a_ab = [jnp.where(strict, x[0:SR, 0:SR], 0.0) for x in gram]
    a_ak = [jnp.where(strict, x[0:SR, SR:2 * SR], 0.0) for x in gram]
    m_rb = [jnp.where(incl, x[SR:2 * SR, 0:SR], 0.0) for x in gram]
    m_rk = [jnp.where(incl, x[SR:2 * SR, SR:2 * SR], 0.0) for x in gram]
    t_inv = [eye_sr + x for x in a_ab]
    power = a_ab
    for _ in range(int(np.log2(L)) - 1):
        power = each(_dot, power, power)
        t_inv = each(lambda t, p: t + _dot(t, p), t_inv, power)
    akv = each(_dot, a_ak, vst)
    au = each(lambda t, a, x: _dot(t, jnp.concatenate([a, x], axis=1)), t_inv, ast, akv)
    ahat = [x[:, 0:PW] for x in au]
    uhat = [x[:, PW:2 * PW] for x in au]
    rhat = each(lambda r, m, a: r + _dot(m, a), rst, m_rb, ahat)
    y0 = each(lambda mb, u, mk, x: _dot(mb, u) + _dot(mk, x), m_rb, uhat, m_rk, vst)
    w_last = [w_in[(c + 1) * L - 1:(c + 1) * L, pr * PW:(pr + 1) * PW] for c, pr in chains]
    p_mat = each(lambda a, b, w: (eye_pw + _dot_tn(a, b)) * w, ahat, bst, w_last)
    q_mat = each(lambda u, x, y, w: _dot_tn(jnp.concatenate([u, x], axis=0), y) * w, uhat, vst, bk, w_last)

    y_chunks = []
    for c in range(n_sub):
        idx = range(c * n_pair, (c + 1) * n_pair)
        s_old = [state[pr] for pr in range(n_pair)]
        y_st = [_dot_nt(rhat[i], s) + y0[i] for i, s in zip(idx, s_old)]
        for pr, i in enumerate(idx):
            state[pr] = _dot(s_old[pr], p_mat[i]) + q_mat[i]
        y_chunks.append(jnp.concatenate([x[0:L] + x[L:SR] for x in y_st], axis=1))
    y = jnp.concatenate(y_chunks, axis=0)
    inv_n = 1.0 / N
    mean = _head_sum(y, bd_ref) * inv_n
    yc = y - mean
    var = _head_sum(yc * yc, bd_ref) * inv_n
    yn = yc * lax.rsqrt(var + LNX_EPS) * lg_ref[...] + lb_ref[...]
    r, k, v = r_ref[...], k_ref[...], v_ref[...]
    bonus = _head_sum(r * k * rk_ref[...], bd_ref) * v
    o_ref[...] = ((yn + bonus) * g_ref[...]).astype(o_ref.dtype)


def _rwkv_scan(r, ld, k, v, kn, ba, g, r_k, lnx_g, lnx_b):
    B, S, W = r.shape
    L = min(RWKV_STEP, S)
    tok = pl.BlockSpec((None, L, W), lambda b, i: (b, i, 0))
    row = lambda z: z.reshape(1, -1)
    return pl.pallas_call(
        _rwkv_scan_kernel,
        grid=(B, S // L),
        in_specs=[tok] * 7 + [_const_spec((1, W))] * 3 + [_const_spec((W, W))],
        out_specs=tok,
        out_shape=jax.ShapeDtypeStruct((B, S, W), BF16),
        scratch_shapes=[pltpu.VMEM((W // (2 * RWKV_HEAD_DIM), 2 * RWKV_HEAD_DIM, 2 * RWKV_HEAD_DIM), F32)],
        compiler_params=_cparams("parallel", "arbitrary"),
        name="rwkv_scan",
    )(r, ld, k, v, kn, ba, g, row(r_k), row(lnx_g), row(lnx_b), _head_block_diag())


def _rwkv_mixer(p, mu, w0, w2, a0, a2, g2, k_k, k_a, r_k, lnx_g, lnx_b, v_first, v_res):
    r, ld, k, v, kn, ba, g = _rwkv_prep(p, mu, w0, w2, a0, a2, g2, k_k, k_a, v_first, v_res)
    y = _rwkv_scan(r, ld, k, v, kn, ba, g, r_k, lnx_g, lnx_b)
    return y, (v if v_res is None else v_first)


def _merge_kernel(x_ref, mod_ref, pg_ref, qg_ref, ya_ref, yb_ref, yc_ref, wg_ref,
                  wa_ref, wb_ref, wc_ref, ow_ref, o_ref):
    x = x_ref[...]
    D = x.shape[1]
    h = _pre(x, pg_ref[...], mod_ref).astype(BF16)
    merged = None
    for i, (y_ref, w_ref) in enumerate(((ya_ref, wa_ref), (yb_ref, wb_ref), (yc_ref, wc_ref))):
        gate = _sigmoid(jnp.dot(h, wg_ref[:, i * D:(i + 1) * D], preferred_element_type=F32))
        term = gate * jnp.dot(y_ref[...], w_ref[...], preferred_element_type=F32)
        merged = term if merged is None else merged + term
    y = _dot(merged, ow_ref[...])
    o_ref[...] = x + mod_ref[2:3, :] * _rms(y, qg_ref[...])


def _merge(x, mod3, pre_g, post_g, y_a, y_b, y_c, w_gate, branch_w, out_w):
    B, S, D = x.shape
    tm = min(FFN_TILE, S)
    tok = lambda n: pl.BlockSpec((None, tm, n), lambda b, i: (b, i, 0))
    w_a = jnp.pad(branch_w[0].reshape(NSA_HEADS, NSA_DK, D), ((0, 0), (0, NSA_VAUG - NSA_DK), (0, 0)))
    w_a = w_a.reshape(NSA_HEADS * NSA_VAUG, D)
    widths = (y_a.shape[-1], y_b.shape[-1], y_c.shape[-1])
    return pl.pallas_call(
        _merge_kernel,
        grid=(B, S // tm),
        in_specs=[tok(D), pl.BlockSpec((None, 3, D), lambda b, i: (b, 0, 0)),
                  _const_spec((1, D)), _const_spec((1, D))]
                 + [tok(n) for n in widths]
                 + [_const_spec((D, N_BRANCH * D))] + [_const_spec((n, D)) for n in widths]
                 + [_const_spec((D, D))],
        out_specs=tok(D),
        out_shape=jax.ShapeDtypeStruct((B, S, D), F32),
        compiler_params=_cparams("parallel", "parallel"),
        name="merge",
    )(x, mod3, pre_g.reshape(1, D), post_g.reshape(1, D), y_a, y_b, y_c, w_gate,
      w_a, branch_w[1], branch_w[2], out_w)


def kernel(x, c, ada_w, ada_b, pre_g, post_g, ffn_w_in, ffn_w_out, mix_w_in, branch_w, out_w,
           cmp_k_w1, cmp_k_w2, cmp_k_pe, cmp_v_w1, cmp_v_w2, cmp_v_pe,
           sgu_ln_g, sgu_ln_b, sgu_w, sgu_b,
           rwkv_mu, rwkv_w0, rwkv_w2, rwkv_a0, rwkv_a2, rwkv_g2, rwkv_kk, rwkv_ka, rwkv_rk,
           rwkv_lnx_g, rwkv_lnx_b, rwkv_v0, rwkv_v1, rwkv_v2):
    B, S, D = x.shape
    depth = ada_w.shape[0]
    mod = _ada_mod(c, ada_w, ada_b).reshape(depth, B, 3, 3, D)
    v_first = None
    for l in range(depth):
        x = _ffn(x, mod[l, :, 0], pre_g[l, 0], post_g[l, 0],
                 ffn_w_in[l, 0].astype(BF16), ffn_w_out[l, 0].astype(BF16))

        w = mix_w_in[l]
        n_in = NSA_COLS + 2 * SGU_WIDTH + RWKV_COLS
        q, kvc, vs, vw, gt, p_sgu, p_rwkv, ks, kw = _mix_proj(
            x, mod[l, :, 1], pre_g[l, 1], *_proj_weights(w))
        y_a = _nsa_mixer(q, kvc, vs, vw, gt, ks, kw, cmp_k_w1[l], cmp_k_w2[l], cmp_k_pe[l],
                         cmp_v_w1[l], cmp_v_w2[l], cmp_v_pe[l])
        y_b = _sgu_mixer(p_sgu, sgu_ln_g[l], sgu_ln_b[l], sgu_w[l], sgu_b[l])
        v_res = None if l == 0 else (rwkv_v0[l - 1], rwkv_v1[l - 1], rwkv_v2[l - 1])
        y_c, v_first = _rwkv_mixer(p_rwkv, rwkv_mu[l], rwkv_w0[l], rwkv_w2[l], rwkv_a0[l], rwkv_a2[l],
                                   rwkv_g2[l], rwkv_kk[l], rwkv_ka[l], rwkv_rk[l],
                                   rwkv_lnx_g[l], rwkv_lnx_b[l], v_first, v_res)
        x = _merge(x, mod[l, :, 1], pre_g[l, 1], post_g[l, 1], y_a, y_b, y_c,
                   w[:, n_in:].astype(BF16), branch_w[l].astype(BF16), out_w[l].astype(BF16))

        x = _ffn(x, mod[l, :, 2], pre_g[l, 2], post_g[l, 2],
                 ffn_w_in[l, 1].astype(BF16), ffn_w_out[l, 1].astype(BF16))
    return x
```

```python
import functools

import numpy as np
import jax
import jax.numpy as jnp
from jax import lax
from jax.experimental import pallas as pl
from jax.experimental.pallas import tpu as pltpu

D_MODEL = 1024
DEPTH = 2
NSA_HEADS = 8
NSA_GROUPS = 2
NSA_HPG = NSA_HEADS // NSA_GROUPS
NSA_DK = 64
CMP_LEN = 32
CMP_STRIDE = 16
CMP_HID = 256
SLC_BLOCK = 64
SLC_TOPN = 16
WINDOW = 512
Q_BLOCK = 128
SGU_CHUNK = 128
SGU_GROUPS = 4
SGU_WIDTH = 512
RWKV_HEADS = 8
RWKV_HEAD_DIM = 64
RWKV_WIDTH = RWKV_HEADS * RWKV_HEAD_DIM
DECAY_LORA = 64
AAA_LORA = 64
MV_LORA = 32
GATE_LORA = 128
N_BRANCH = 3
BRANCH_WIDTH = 512
D_FF = 2816
MACARON_WEIGHT = 0.5
RMS_EPS = 1e-6
LN_EPS = 1e-5
LNX_EPS = 64e-5
NEG_INF = -1e30
FORCE_SCORE = 1e4

NSA_COLS = NSA_HEADS * NSA_DK + 6 * NSA_GROUPS * NSA_DK + 3 * NSA_HEADS
RWKV_COLS = 3 * RWKV_WIDTH + DECAY_LORA + AAA_LORA + GATE_LORA
Q_COLS = NSA_HEADS * NSA_DK
KV_COLS = 6 * NSA_GROUPS * NSA_DK
GT_COLS = 3 * NSA_HEADS
LANES = 128
NSA_STEP = 256
NSA_VAUG = 128
NSA_AUG = 128
POS_SPLIT = 64
MASK_BIG = 2.0 ** 100

BF16 = jnp.bfloat16
F32 = jnp.float32

FFN_TILE = 512
FFN_CHUNK = 1408
SEL_TILE = 512
RWKV_CHUNK = 64
RWKV_PACK = 4
RWKV_STEP = 256
SGU_TILE = 512
VMEM_LIMIT = 56 * 1024 * 1024


def _cparams(*sem):
    return pltpu.CompilerParams(dimension_semantics=sem, vmem_limit_bytes=VMEM_LIMIT)


def _dot(a, b):
    return jnp.dot(a.astype(BF16), b.astype(BF16), preferred_element_type=F32)


def _dot_nt(a, b):
    return lax.dot_general(a.astype(BF16), b.astype(BF16), (((1,), (1,)), ((), ())),
                           preferred_element_type=F32)


def _dot_tn(a, b):
    return lax.dot_general(a.astype(BF16), b.astype(BF16), (((0,), (0,)), ((), ())),
                           preferred_element_type=F32)


def _split(x):
    hi = x.astype(BF16)
    lo = (x - hi.astype(F32)).astype(BF16)
    return hi, lo


def _dot_x2(a, b):
    hi, lo = _split(a)
    return _dot(hi, b) + _dot(lo, b)


def _sigmoid(x):
    return 1.0 / (1.0 + jnp.exp(-x))


def _silu(x):
    return x * _sigmoid(x)


def _rms(x, g):
    return x * lax.rsqrt(jnp.mean(x * x, axis=-1, keepdims=True) + RMS_EPS) * g


def _pre(x, g, mod_ref):
    return _rms(x, g) * (1.0 + mod_ref[1:2, :]) + mod_ref[0:1, :]


def _ada_kernel(c_ref, w_ref, b_ref, o_ref):
    cond = _silu(c_ref[...])
    o_ref[...] = _dot(cond, w_ref[...]) + b_ref[...]


def _ada_mod(c, ada_w, ada_b):
    L, D, N = ada_w.shape
    B = c.shape[0]
    tn = 1536
    return pl.pallas_call(
        _ada_kernel,
        grid=(L, N // tn),
        in_specs=[
            pl.BlockSpec((B, D), lambda l, j: (0, 0)),
            pl.BlockSpec((None, D, tn), lambda l, j: (l, 0, j)),
            pl.BlockSpec((None, 1, tn), lambda l, j: (l, 0, j)),
        ],
        out_specs=pl.BlockSpec((None, B, tn), lambda l, j: (l, 0, j)),
        out_shape=jax.ShapeDtypeStruct((L, B, N), F32),
        compiler_params=_cparams("arbitrary", "arbitrary"),
        name="ada_mod",
    )(c, ada_w, ada_b.reshape(L, 1, N))


def _ffn_kernel(x_ref, mod_ref, pg_ref, qg_ref, win_ref, wout_ref, o_ref):
    x = x_ref[...]
    h = _pre(x, pg_ref[...], mod_ref).astype(BF16)
    acc = None
    for c in range(D_FF // FFN_CHUNK):
        lo, hi = c * FFN_CHUNK, (c + 1) * FFN_CHUNK
        gate = jnp.dot(h, win_ref[:, lo:hi], preferred_element_type=F32)
        up = jnp.dot(h, win_ref[:, D_FF + lo:D_FF + hi], preferred_element_type=F32)
        act = (_silu(gate) * up).astype(BF16)
        part = jnp.dot(act, wout_ref[lo:hi, :], preferred_element_type=F32)
        acc = part if acc is None else acc + part
    o_ref[...] = x + MACARON_WEIGHT * mod_ref[2:3, :] * _rms(acc, qg_ref[...])


def _const_spec(shape):
    nd = len(shape)
    return pl.BlockSpec(shape, lambda *_: (0,) * nd, pipeline_mode=pl.Buffered(1))


def _ffn(x, mod3, pre_g, post_g, w_in, w_out):
    B, S, D = x.shape
    tm = min(FFN_TILE, S)
    return pl.pallas_call(
        _ffn_kernel,
        grid=(B, S // tm),
        in_specs=[
            pl.BlockSpec((None, tm, D), lambda b, i: (b, i, 0)),
            pl.BlockSpec((None, 3, D), lambda b, i: (b, 0, 0)),
            _const_spec((1, D)),
            _const_spec((1, D)),
            _const_spec((D, 2 * D_FF)),
            _const_spec((D_FF, D)),
        ],
        out_specs=pl.BlockSpec((None, tm, D), lambda b, i: (b, i, 0)),
        out_shape=jax.ShapeDtypeStruct((B, S, D), F32),
        compiler_params=_cparams("parallel", "parallel"),
        name="ffn",
    )(x, mod3, pre_g.reshape(1, D), post_g.reshape(1, D), w_in, w_out)


Q_AUG_COLS = NSA_HEADS * NSA_AUG
V_AUG_COLS = NSA_GROUPS * NSA_VAUG
CMP_COLS = 2 * NSA_GROUPS * NSA_DK
PROJ_WIDTHS = (Q_AUG_COLS, CMP_COLS, V_AUG_COLS, V_AUG_COLS, LANES, 2 * SGU_WIDTH, RWKV_COLS)
PROJ_COLS = sum(PROJ_WIDTHS)
KT_ROWS = 2 * NSA_GROUPS * NSA_AUG


def _proj_kernel(x_ref, mod_ref, pg_ref, w_ref, wkt_ref, qc_ref, vc_ref,
                 q_ref, cmp_ref, vs_ref, vw_ref, gt_ref, sgu_ref, rw_ref, ks_ref, kw_ref):
    h = _pre(x_ref[...], pg_ref[...], mod_ref).astype(BF16)
    tm = h.shape[0]
    edges = np.cumsum((0,) + PROJ_WIDTHS)
    cols = [jnp.dot(h, w_ref[:, int(a):int(b)], preferred_element_type=F32)
            for a, b in zip(edges[:-1], edges[1:])]
    q_ref[...] = (cols[0] * NSA_DK ** -0.5 + qc_ref[...]).astype(BF16)
    cmp_ref[...] = cols[1].astype(BF16)
    vs_ref[...] = (cols[2] + vc_ref[...]).astype(BF16)
    vw_ref[...] = (cols[3] + vc_ref[...]).astype(BF16)
    gt_ref[...] = cols[4]
    sgu_ref[...] = cols[5]
    rw_ref[...] = cols[6]
    kt = _dot_nt(wkt_ref[...], h)
    pos = pl.program_id(1) * tm + lax.broadcasted_iota(jnp.int32, (1, tm), 1)
    row = lax.broadcasted_iota(jnp.int32, (NSA_AUG, 1), 0)
    pos_rows = jnp.where(row == NSA_DK, (pos // POS_SPLIT).astype(F32),
                         jnp.where(row == NSA_DK + 1, (pos % POS_SPLIT).astype(F32), 0.0))
    for g in range(NSA_GROUPS):
        ks_ref[g, 0] = (kt[g * NSA_AUG:(g + 1) * NSA_AUG] + pos_rows).astype(BF16)
        kw = (kt[(NSA_GROUPS + g) * NSA_AUG:(NSA_GROUPS + g + 1) * NSA_AUG] + pos_rows).astype(BF16)
        for t in range(tm // LANES):
            kw_ref[g, t] = kw[:, t * LANES:(t + 1) * LANES]


def _proj_weights(w):
    D = w.shape[0]
    G, DK = NSA_GROUPS, NSA_DK
    q, kc, vc, ks, vs, kw, vw = [w[:, a:b] for a, b in zip(
        np.cumsum((0, Q_COLS) + (G * DK,) * 5), np.cumsum((Q_COLS,) + (G * DK,) * 6))]
    gt = w[:, Q_COLS + KV_COLS:NSA_COLS]

    def padded(z, width):
        n = z.shape[1] // DK
        return jnp.pad(z.reshape(D, n, DK), ((0, 0), (0, 0), (0, width - DK))).reshape(D, n * width)

    token_major = jnp.concatenate(
        [padded(q, NSA_AUG), kc, vc, padded(vs, NSA_VAUG), padded(vw, NSA_VAUG),
         jnp.pad(gt, ((0, 0), (0, LANES - GT_COLS))), w[:, NSA_COLS:NSA_COLS + 2 * SGU_WIDTH + RWKV_COLS]],
        axis=1).astype(BF16)
    keys_t = jnp.concatenate([padded(ks, NSA_AUG), padded(kw, NSA_AUG)], axis=1).T.astype(BF16)
    return token_major, keys_t


def _mix_proj(x, mod3, pre_g, w, wkt):
    B, S, D = x.shape
    tm = min(FFN_TILE, S)
    slopes = 2.0 ** -np.arange(1, NSA_HEADS + 1, dtype=np.float32)
    q_const = np.zeros((NSA_HEADS, NSA_AUG), np.float32)
    q_const[:, NSA_DK] = POS_SPLIT * slopes
    q_const[:, NSA_DK + 1] = slopes
    v_const = np.zeros((NSA_GROUPS, NSA_VAUG), np.float32)
    v_const[:, NSA_DK] = 1.0
    dtypes = (BF16, BF16, BF16, BF16, F32, F32, F32)
    tok = lambda n: pl.BlockSpec((None, tm, n), lambda b, i: (b, i, 0))
    return pl.pallas_call(
        _proj_kernel,
        grid=(B, S // tm),
        in_specs=[
            tok(D),
            pl.BlockSpec((None, 3, D), lambda b, i: (b, 0, 0)),
            _const_spec((1, D)),
            _const_spec((D, PROJ_COLS)),
            _const_spec((KT_ROWS, D)),
            _const_spec((1, Q_AUG_COLS)),
            _const_spec((1, V_AUG_COLS)),
        ],
        out_specs=[tok(n) for n in PROJ_WIDTHS] + [
            pl.BlockSpec((None, NSA_GROUPS, 1, NSA_AUG, tm), lambda b, i: (b, 0, i, 0, 0)),
            pl.BlockSpec((None, NSA_GROUPS, tm // LANES, NSA_AUG, LANES), lambda b, i: (b, 0, i, 0, 0))],
        out_shape=[jax.ShapeDtypeStruct((B, S, n), dt) for n, dt in zip(PROJ_WIDTHS, dtypes)] + [
            jax.ShapeDtypeStruct((B, NSA_GROUPS, S // tm, NSA_AUG, tm), BF16),
            jax.ShapeDtypeStruct((B, NSA_GROUPS, S // LANES, NSA_AUG, LANES), BF16)],
        compiler_params=_cparams("parallel", "parallel"),
        name="mix_proj",
    )(x, mod3, pre_g.reshape(1, D), w, wkt,
      jnp.asarray(q_const.reshape(1, -1)), jnp.asarray(v_const.reshape(1, -1)))


def _cmp_kernel(xk_ref, xv_ref, pek_ref, pev_ref, w1k_ref, w1v_ref, w2kt_ref, w2v_ref,
                kct_ref, vc_ref):
    half = (CMP_LEN // 2) * NSA_DK

    def hidden(x_ref, pe_ref, w1_ref):
        x = x_ref[...].astype(F32)
        n = x.shape[0]
        top = _dot(x + pe_ref[0:1, :], w1_ref[0:half, :])
        bot = _dot(x + pe_ref[1:2, :], w1_ref[half:2 * half, :])
        return _silu(top + pltpu.roll(bot, n - 1, axis=0))

    kct_ref[...] = _dot_nt(w2kt_ref[...], hidden(xk_ref, pek_ref, w1k_ref)).astype(BF16)
    vc_ref[...] = _dot(hidden(xv_ref, pev_ref, w1v_ref), w2v_ref[...]).astype(BF16)


def _compress(xk, xv, pe_k, pe_v, w1k, w1v, w2kt, w2v):
    B, G, NC, F = xk.shape
    blk = lambda *s: pl.BlockSpec((None, None) + s, lambda b, g: (b, g, 0, 0))
    return pl.pallas_call(
        _cmp_kernel,
        grid=(B, G),
        in_specs=[blk(NC, F), blk(NC, F), _const_spec((2, F)), _const_spec((2, F)),
                  _const_spec((2 * F, CMP_HID)), _const_spec((2 * F, CMP_HID)),
                  _const_spec((NSA_DK, CMP_HID)), _const_spec((CMP_HID, NSA_VAUG))],
        out_specs=[blk(NSA_DK, NC), blk(NC, NSA_VAUG)],
        out_shape=[jax.ShapeDtypeStruct((B, G, NSA_DK, NC), BF16),
                   jax.ShapeDtypeStruct((B, G, NC, NSA_VAUG), BF16)],
        compiler_params=_cparams("parallel", "parallel"),
        name="nsa_compress",
    )(xk, xv, pe_k, pe_v, w1k, w1v, w2kt, w2v)


def _masked_softmax(s, valid):
    s = jnp.where(valid, s, NEG_INF)
    m = jnp.max(s, axis=-1, keepdims=True)
    p = jnp.where(valid, jnp.exp(s - m), 0.0)
    l = jnp.sum(p, axis=-1, keepdims=True)
    return p * (1.0 / jnp.where(l > 0.0, l, 1.0))


def _nsa_kernel(q_ref, gt_ref, kc_ref, vc_ref, ks_ref, vs_ref, kw_ref, vw_ref, selt_ref, e_ref,
                o_ref, imp_ref, *, n_top, tk):
    s0 = pl.program_id(2) * NSA_STEP
    R = NSA_HPG * Q_BLOCK
    n_slc = selt_ref.shape[0]
    n_cmp = kc_ref.shape[-1]
    parts = range(NSA_STEP // Q_BLOCK)
    each = lambda fn, *lists: [fn(*xs) for xs in zip(*lists)]
    rows = lambda c: slice(c * Q_BLOCK, (c + 1) * Q_BLOCK)
    head_cols = lambda h: slice(h * NSA_AUG, (h + 1) * NSA_AUG)
    q = [jnp.concatenate([q_ref[rows(c), head_cols(h)] for h in range(NSA_HPG)], axis=0) for c in parts]
    q_row = lax.broadcasted_iota(jnp.int32, (R, 1), 0) & (Q_BLOCK - 1)
    t_i = [s0 + c * Q_BLOCK + q_row for c in parts]

    def normalised(acc):
        return acc * (1.0 / acc[:, NSA_DK:NSA_DK + 1])

    cmp_end = lax.broadcasted_iota(jnp.int32, (1, n_cmp), 1) * CMP_STRIDE + (CMP_LEN - 1)
    s_c = [jnp.dot(x, kc_ref[...], preferred_element_type=F32) for x in q]
    p_c = each(lambda s, t: _masked_softmax(s, cmp_end <= t), s_c, t_i)
    o_c = [_dot(p, vc_ref[...]) for p in p_c]

    j = lax.broadcasted_iota(jnp.int32, (n_slc, Q_BLOCK), 0)
    imp = []
    for c in parts:
        p_sum = p_c[c][0:Q_BLOCK]
        for h in range(1, NSA_HPG):
            p_sum = p_sum + p_c[c][h * Q_BLOCK:(h + 1) * Q_BLOCK]
        hi, lo = _split(p_sum)
        raw = _dot_nt(selt_ref[...], hi) + _dot_nt(selt_ref[...], lo)
        tq = s0 + c * Q_BLOCK + lax.broadcasted_iota(jnp.int32, (n_slc, Q_BLOCK), 1)
        cur = tq // SLC_BLOCK
        forced = (j == 0) | (j == cur) | (j == cur - 1)
        live = j * SLC_BLOCK <= tq
        imp.append(jnp.where(forced, FORCE_SCORE, jnp.where(live, raw, NEG_INF)))
        imp_ref[c] = imp[c]

    def rank_body(i, cnt):
        tie = jnp.where(j > i, 1.0, 0.0)
        out = []
        for c in parts:
            r = imp_ref[c, pl.ds(i, 1), :]
            out.append(cnt[c] + jnp.where(r > imp[c], 1.0, jnp.where(r == imp[c], tie, 0.0)))
        return tuple(out)

    n_live = (s0 + NSA_STEP) // SLC_BLOCK
    cnt = lax.fori_loop(0, n_live, rank_body, tuple(jnp.zeros((n_slc, Q_BLOCK), F32) for _ in parts))
    lhs = []
    for c in parts:
        dropped = jnp.concatenate(
            [jnp.where(cnt[c] < n_top, 0.0, 1.0), jnp.zeros((NSA_AUG - n_slc, Q_BLOCK), F32)], axis=0)
        dropped = dropped.T.astype(BF16)
        lhs.append(jnp.concatenate([q[c], jnp.concatenate([dropped] * NSA_HPG, axis=0)], axis=1))

    def sel_tile(kt, carry, diagonal):
        m, acc = carry[0::2], carry[1::2]
        rhs = jnp.concatenate([ks_ref[kt], e_ref[kt]], axis=0)
        v = vs_ref[pl.ds(pl.multiple_of(kt * tk, tk), tk), :]
        s = [jnp.dot(x, rhs, preferred_element_type=F32) for x in lhs]
        if diagonal:
            pos = kt * tk + lax.broadcasted_iota(jnp.int32, (1, tk), 1)
            s = each(lambda x, t: jnp.where(pos <= t, x, NEG_INF), s, t_i)
        m_new = each(lambda x, y: jnp.maximum(x, jnp.max(y, axis=-1, keepdims=True)), m, s)
        alpha = each(lambda x, y: jnp.exp(x - y), m, m_new)
        p = each(lambda x, y: jnp.exp(x - y).astype(BF16), s, m_new)
        acc = each(lambda a, x, y: a * x + jnp.dot(y, v, preferred_element_type=F32), alpha, acc, p)
        out = []
        for x, y in zip(m_new, acc):
            out += [x, y]
        return tuple(out)

    last = (s0 + NSA_STEP - 1) // tk
    init = []
    for _ in parts:
        init += [jnp.full((R, 1), NEG_INF, F32), jnp.zeros((R, NSA_VAUG), F32)]
    carry = lax.fori_loop(0, last, functools.partial(sel_tile, diagonal=False), tuple(init))
    o_s = [normalised(x) for x in sel_tile(last, carry, True)[1::2]]

    n_wt = (WINDOW + Q_BLOCK) // LANES
    lane_pos = lax.broadcasted_iota(jnp.int32, (1, LANES), 1)
    s_w, start = [], []
    for c in parts:
        start.append(jnp.maximum(s0 + c * Q_BLOCK - WINDOW, 0))
        tiles = []
        for i in range(n_wt):
            pos = start[c] + i * LANES + lane_pos
            x = jnp.dot(q[c], kw_ref[start[c] // LANES + i], preferred_element_type=F32)
            x = jnp.where(pos <= t_i[c], x, NEG_INF)
            if i == 0:
                x = jnp.where(pos > t_i[c] - WINDOW, x, NEG_INF)
            tiles.append(x)
        s_w.append(jnp.concatenate(tiles, axis=1))
    p_w = [jnp.exp(x - jnp.max(x, axis=-1, keepdims=True)) for x in s_w]
    o_w = [normalised(_dot(p, vw_ref[pl.ds(pl.multiple_of(st, LANES), n_wt * LANES), :]))
           for p, st in zip(p_w, start)]

    for c in parts:
        gate = _sigmoid(gt_ref[:, rows(c), :].reshape(R, 3))
        o = gate[:, 0:1] * o_c[c] + gate[:, 1:2] * o_s[c] + gate[:, 2:3] * o_w[c]
        for h in range(NSA_HPG):
            o_ref[rows(c), head_cols(h)] = o[h * Q_BLOCK:(h + 1) * Q_BLOCK].astype(o_ref.dtype)


def _selection_map_t(n_cmp_rows, n_slc):
    rs, rc = SLC_BLOCK // CMP_STRIDE, CMP_LEN // CMP_STRIDE
    m = np.zeros((n_slc, n_cmp_rows), np.float32)
    for jj in range(n_slc):
        for a in range(rs):
            for b in range(rc):
                i = rs * jj - a - b
                if 0 <= i < n_cmp_rows - 1:
                    m[jj, i] += 1.0
    return jnp.asarray(m, BF16)


def _nsa_attention(q, gt, kc, vc, ks, vs, kw, vw):
    B, S, _ = q.shape
    G, A = NSA_GROUPS, NSA_AUG
    n_slc = S // SLC_BLOCK
    n_cmp = kc.shape[-1]
    n_kt, _, tk = ks.shape[2:]
    selt = _selection_map_t(n_cmp, n_slc)
    key_blk = np.arange(S).reshape(n_kt, 1, tk) // SLC_BLOCK
    e = np.where(np.arange(A).reshape(1, A, 1) == key_blk, -MASK_BIG, 0.0)
    heads = pl.BlockSpec((None, NSA_STEP, NSA_HPG * A), lambda b, g, i: (b, i, g))
    values = pl.BlockSpec((None, S, NSA_VAUG), lambda b, g, i: (b, 0, g))
    per_g = lambda *s: pl.BlockSpec((None, None) + s, lambda b, g, i: (b, g) + (0,) * len(s))
    kern = functools.partial(_nsa_kernel, n_top=min(SLC_TOPN, n_slc), tk=tk)
    return pl.pallas_call(
        kern,
        grid=(B, G, S // NSA_STEP),
        in_specs=[heads, pl.BlockSpec((None, NSA_HPG, NSA_STEP, 3), lambda b, g, i: (b, g, i, 0)),
                  per_g(A, n_cmp), per_g(n_cmp, NSA_VAUG),
                  per_g(n_kt, A, tk), values, per_g(S // LANES, A, LANES), values,
                  pl.BlockSpec((n_slc, n_cmp), lambda b, g, i: (0, 0)),
                  pl.BlockSpec((n_kt, A, tk), lambda b, g, i: (0, 0, 0))],
        out_specs=heads,
        out_shape=jax.ShapeDtypeStruct((B, S, NSA_HEADS * NSA_VAUG), BF16),
        scratch_shapes=[pltpu.VMEM((NSA_STEP // Q_BLOCK, n_slc, Q_BLOCK), F32)],
        compiler_params=_cparams("parallel", "parallel", "arbitrary"),
        name="nsa_attention",
    )(q, gt, kc, vc, ks, vs, kw, vw, selt, jnp.asarray(e, BF16))


def _with_position_rows(kt, pos):
    lead = kt.shape[:-2]
    n = kt.shape[-1]
    hi = jnp.broadcast_to(jnp.asarray(pos // POS_SPLIT, BF16)[..., None, :], lead + (1, n))
    lo = jnp.broadcast_to(jnp.asarray(pos % POS_SPLIT, BF16)[..., None, :], lead + (1, n))
    pad = jnp.zeros(lead + (NSA_AUG - NSA_DK - 2, n), BF16)
    return jnp.concatenate([kt, hi, lo, pad], axis=-2)


def _nsa_mixer(q, kvc, vs, vw, gt, ks, kw, k_w1, k_w2, k_pe, v_w1, v_w2, v_pe):
    B, S, _ = q.shape
    G, DK = NSA_GROUPS, NSA_DK
    half = CMP_LEN // 2
    nc = S // half

    def half_blocks(z):
        z = z.reshape(B, nc, half, G, DK)
        return z.transpose(0, 3, 1, 2, 4).reshape(B, G, nc, half * DK)

    kct, vcc = _compress(
        half_blocks(kvc[:, :, :G * DK]), half_blocks(kvc[:, :, G * DK:]),
        k_pe.reshape(2, half * DK), v_pe.reshape(2, half * DK),
        k_w1.astype(BF16), v_w1.astype(BF16), k_w2.T.astype(BF16),
        jnp.pad(v_w2, ((0, 0), (0, NSA_VAUG - DK))).astype(BF16))
    gth = gt[:, :, :GT_COLS].reshape(B, S, NSA_HEADS, 3).transpose(0, 2, 1, 3)
    kca = _with_position_rows(kct, np.arange(nc) * CMP_STRIDE + (CMP_LEN - 1))
    return _nsa_attention(q, gth, kca, vcc, ks, vs, kw, vw)


def _gelu_tanh(x):
    return 0.5 * x * (1.0 + jnp.tanh(np.sqrt(2.0 / np.pi).astype(np.float32) * (x + 0.044715 * (x * x * x))))


def _sgu_kernel(p_ref, g_ref, b_ref, w_ref, bs_ref, o_ref):
    ts = p_ref.shape[0]
    gd = SGU_WIDTH // SGU_GROUPS
    u = _gelu_tanh(p_ref[:, 0:SGU_WIDTH])
    v = _gelu_tanh(p_ref[:, SGU_WIDTH:2 * SGU_WIDTH])
    mu = jnp.mean(v, axis=-1, keepdims=True)
    vc = v - mu
    var = jnp.mean(vc * vc, axis=-1, keepdims=True)
    vn = (vc * lax.rsqrt(var + LN_EPS) * g_ref[...] + b_ref[...]).astype(BF16)
    causal = (lax.broadcasted_iota(jnp.int32, (SGU_CHUNK, SGU_CHUNK), 0)
              >= lax.broadcasted_iota(jnp.int32, (SGU_CHUNK, SGU_CHUNK), 1))
    ws = [jnp.where(causal, w_ref[i], 0.0).astype(BF16) for i in range(SGU_GROUPS)]
    for n in range(ts // SGU_CHUNK):
        r0, r1 = n * SGU_CHUNK, (n + 1) * SGU_CHUNK
        s = jnp.concatenate(
            [jnp.dot(ws[i], vn[r0:r1, i * gd:(i + 1) * gd], preferred_element_type=F32)
             for i in range(SGU_GROUPS)], axis=1) + bs_ref[...]
        o_ref[r0:r1, :] = (u[r0:r1] * s).astype(o_ref.dtype)


def _sgu_mixer(p, ln_g, ln_b, w_s, b_s):
    B, S, _ = p.shape
    ts = min(SGU_TILE, S)
    gd = SGU_WIDTH // SGU_GROUPS
    bias = jnp.repeat(b_s.T, gd, axis=1)
    return pl.pallas_call(
        _sgu_kernel,
        grid=(B, S // ts),
        in_specs=[pl.BlockSpec((None, ts, 2 * SGU_WIDTH), lambda b, i: (b, i, 0)),
                  _const_spec((1, SGU_WIDTH)), _const_spec((1, SGU_WIDTH)),
                  _const_spec((SGU_GROUPS, SGU_CHUNK, SGU_CHUNK)), _const_spec((SGU_CHUNK, SGU_WIDTH))],
        out_specs=pl.BlockSpec((None, ts, SGU_WIDTH), lambda b, i: (b, i, 0)),
        out_shape=jax.ShapeDtypeStruct((B, S, SGU_WIDTH), BF16),
        compiler_params=_cparams("parallel", "parallel"),
        name="sgu",
    )(p, ln_g.reshape(1, -1), ln_b.reshape(1, -1), w_s, bias)


def _head_sum(x, bd_ref):
    return _dot_x2(x, bd_ref[...])


def _rwkv_operands(p_ref, carry, mu_ref, w0_ref, w2_ref, a0_ref, a2_ref, g2_ref, kk_ref, ka_ref, bd_ref,
                   v_res_refs):
    W = RWKV_WIDTH
    p = p_ref[...]
    tr = p.shape[0]

    @pl.when(pl.program_id(1) == 0)
    def _():
        carry[...] = jnp.zeros_like(carry)

    rowi = lax.broadcasted_iota(jnp.int32, (tr, 1), 0)
    p_prev = jnp.where(rowi == 0, carry[0:1, :], pltpu.roll(p, 1, axis=0))
    carry[0:1, :] = p[tr - 1:tr, :]
    ps = p + (p_prev - p) * mu_ref[...]
    r, k, v = ps[:, 0:W], ps[:, W:2 * W], ps[:, 2 * W:3 * W]
    wa = ps[:, 3 * W:3 * W + DECAY_LORA + AAA_LORA]
    gd = ps[:, 3 * W + DECAY_LORA + AAA_LORA:]
    w = w0_ref[...] + _dot(jnp.tanh(wa), w2_ref[...])
    x = -w
    softplus = jnp.maximum(x, 0.0) + jnp.log(1.0 + jnp.exp(-jnp.abs(x)))
    ld = -jnp.exp(-softplus - 0.5)
    a = _sigmoid(a0_ref[...] + _dot(wa, a2_ref[...]))
    g = _dot(_sigmoid(gd), g2_ref[...])
    if v_res_refs is not None:
        vf_ref, v0_ref, v1_ref, v2_ref = v_res_refs
        lora = _dot(_dot(v, v1_ref[...]), v2_ref[...])
        v = v + (vf_ref[...] - v) * _sigmoid(v0_ref[...] + lora)
    kk = k * kk_ref[...]
    norm = jnp.sqrt(_head_sum(kk * kk, bd_ref))
    kn = kk * (1.0 / jnp.maximum(norm, 1e-12))
    return r, ld, k * (1.0 + (a - 1.0) * ka_ref[...]), v, kn, kn * a, g


def _head_block_diag():
    idx = np.arange(RWKV_WIDTH) // RWKV_HEAD_DIM
    return jnp.asarray(idx[:, None] == idx[None, :], BF16)


def _rwkv_kernel(*refs, has_vres):
    (p_ref, mu_ref, w0_ref, w2_ref, a0_ref, a2_ref, g2_ref, kk_ref, ka_ref,
     rk_ref, lg_ref, lb_ref, bd_ref) = refs[:13]
    if has_vres:
        v_res_refs = refs[13:17]
        o_ref, state, carry = refs[17:]
    else:
        v_res_refs = None
        o_ref, v_ref, state, carry = refs[13:]
    r, ld, k, v, kn, ba, g = _rwkv_operands(p_ref, carry, mu_ref, w0_ref, w2_ref, a0_ref, a2_ref, g2_ref,
                                            kk_ref, ka_ref, bd_ref, v_res_refs)
    if not has_vres:
        v_ref[...] = v
    L = RWKV_CHUNK
    N = RWKV_HEAD_DIM
    NH = RWKV_PACK
    QW = NH * N
    n_sub = r.shape[0] // L

    @pl.when(pl.program_id(1) == 0)
    def _():
        state[...] = jnp.zeros_like(state)

    tt = lax.broadcasted_iota(jnp.int32, (n_sub * L, n_sub * L), 0)
    ts = lax.broadcasted_iota(jnp.int32, (n_sub * L, n_sub * L), 1)
    tril_ones = jnp.where((tt >= ts) & (tt // L == ts // L), 1.0, 0.0).astype(BF16)
    own = (lax.broadcasted_iota(jnp.int32, (QW, QW), 0) // N
           == lax.broadcasted_iota(jnp.int32, (QW, QW), 1) // N)
    ti = lax.broadcasted_iota(jnp.int32, (L, QW), 0)
    si = lax.broadcasted_iota(jnp.int32, (L, QW), 1) % L
    strict = ti > si
    incl = ti >= si
    eye_tok = jnp.where(ti == si, 1.0, 0.0)
    eye_ch = jnp.where(lax.broadcasted_iota(jnp.int32, (QW, QW), 0)
                       == lax.broadcasted_iota(jnp.int32, (QW, QW), 1), 1.0, 0.0)

    def stacked(x):
        x = x.astype(BF16)
        return jnp.where(own, jnp.concatenate([x] * NH, axis=0), jnp.zeros((), BF16))

    h1 = ld.astype(BF16)
    r1 = ld - h1.astype(F32)
    h2 = r1.astype(BF16)
    h3 = (r1 - h2.astype(F32)).astype(BF16)
    cum = _dot(tril_ones, h1) + (_dot(tril_ones, h2) + _dot(tril_ones, h3))
    w_in = jnp.exp(cum)
    w_inv = jnp.exp(-cum)
    rt = r * w_in
    kt = k * w_inv
    at = -kn * jnp.exp(cum - ld)
    bt = ba * w_inv

    n_grp = RWKV_WIDTH // QW
    chains = [(c, q) for c in range(n_sub) for q in range(n_grp)]
    each = lambda fn, *lists: [fn(*xs) for xs in zip(*lists)]
    pieces = lambda z: [z[c * L:(c + 1) * L, q * QW:(q + 1) * QW] for c, q in chains]
    cat0 = lambda *xs: jnp.concatenate(xs, axis=0)
    cat1 = lambda *xs: jnp.concatenate(xs, axis=1)
    a_t, r_t, b_t, k_t, v_t = pieces(at), pieces(rt), pieces(bt), pieces(kt), pieces(v)
    v_st = [stacked(x) for x in v_t]
    gram = each(lambda a, r, b, k: _dot_nt(cat0(a, r), cat0(stacked(b), stacked(k))), a_t, r_t, b_t, k_t)
    a_ab = [jnp.where(strict, x[0:L, 0:QW], 0.0) for x in gram]
    a_ak = [jnp.where(strict, x[0:L, QW:2 * QW], 0.0) for x in gram]
    m_rb = [jnp.where(incl, x[L:2 * L, 0:QW], 0.0) for x in gram]
    m_rk = [jnp.where(incl, x[L:2 * L, QW:2 * QW], 0.0) for x in gram]
    t_inv = [eye_tok + x for x in a_ab]
    power = each(lambda p: _dot(p, stacked(p)), a_ab)
    for _ in range(int(np.log2(L)) - 2):
        both = each(lambda p, t: _dot(cat0(p, t), stacked(p)), power, t_inv)
        power = [x[0:L] for x in both]
        t_inv = each(lambda t, x: t + x[L:2 * L], t_inv, both)
    t_inv = each(lambda t, p: t + _dot(t, stacked(p)), t_inv, power)
    akv = each(_dot, a_ak, v_st)
    au = each(lambda t, a, x: _dot(t, cat1(stacked(a), stacked(x))), t_inv, a_t, akv)
    ahat = [x[:, 0:QW] for x in au]
    uhat = [x[:, QW:2 * QW] for x in au]
    rhat = each(lambda r, m, a: r + _dot(m, stacked(a)), r_t, m_rb, ahat)
    y0 = each(lambda mb, mk, u, x: _dot(cat1(mb, mk), cat0(stacked(u), x)), m_rb, m_rk, uhat, v_st)
    w_last = [w_in[(c + 1) * L - 1:(c + 1) * L, q * QW:(q + 1) * QW] for c, q in chains]
    p_mat = each(lambda a, b, w: jnp.where(own, eye_ch + _dot_tn(a, b), 0.0) * w, ahat, b_t, w_last)
    q_mat = each(lambda u, x, b, k, w: jnp.where(own, _dot_tn(cat0(u, x), cat0(b, k)), 0.0) * w,
                 uhat, v_t, b_t, k_t, w_last)

    y_chunks = []
    for c in range(n_sub):
        idx = range(c * n_grp, (c + 1) * n_grp)
        s_old = [state[q] for q in range(n_grp)]
        y_chunks.append(jnp.concatenate([_dot_nt(rhat[i], s) + y0[i] for i, s in zip(idx, s_old)], axis=1))
        for q, i in enumerate(idx):
            state[q] = _dot(s_old[q], p_mat[i]) + q_mat[i]
    y = jnp.concatenate(y_chunks, axis=0)
    inv_n = 1.0 / N
    mean = _head_sum(y, bd_ref) * inv_n
    yc = y - mean
    var = _head_sum(yc * yc, bd_ref) * inv_n
    yn = yc * lax.rsqrt(var + LNX_EPS) * lg_ref[...] + lb_ref[...]
    bonus = _head_sum(r * k * rk_ref[...], bd_ref) * v
    o_ref[...] = ((yn + bonus) * g).astype(o_ref.dtype)


def _rwkv_mixer(p, mu, w0, w2, a0, a2, g2, k_k, k_a, r_k, lnx_g, lnx_b, v_first, v_res):
    B, S, C = p.shape
    W = RWKV_WIDTH
    step = min(RWKV_STEP, S)
    qw = RWKV_PACK * RWKV_HEAD_DIM
    row = lambda z: z.reshape(1, -1)
    lora_rows = DECAY_LORA + AAA_LORA
    w2p = jnp.zeros((lora_rows, W), F32).at[:DECAY_LORA].set(w2).astype(BF16)
    a2p = jnp.zeros((lora_rows, W), F32).at[DECAY_LORA:].set(a2).astype(BF16)
    tok = lambda n: pl.BlockSpec((None, step, n), lambda b, i: (b, i, 0))
    vec = _const_spec((1, W))
    args = [p, row(mu), row(w0), w2p, row(a0), a2p, g2.astype(BF16), row(k_k), row(k_a),
            row(r_k), row(lnx_g), row(lnx_b), _head_block_diag()]
    specs = [tok(C), _const_spec((1, C)), vec, _const_spec((lora_rows, W)), vec,
             _const_spec((lora_rows, W)), _const_spec((GATE_LORA, W)), vec, vec,
             vec, vec, vec, _const_spec((W, W))]
    out_specs = [tok(W)]
    out_shape = [jax.ShapeDtypeStruct((B, S, W), BF16)]
    if v_res is None:
        out_specs.append(tok(W))
        out_shape.append(jax.ShapeDtypeStruct((B, S, W), F32))
    else:
        v0, v1, v2 = v_res
        args += [v_first, row(v0), v1.astype(BF16), v2.astype(BF16)]
        specs += [tok(W), vec, _const_spec((W, MV_LORA)), _const_spec((MV_LORA, W))]
    out = pl.pallas_call(
        functools.partial(_rwkv_kernel, has_vres=v_res is not None),
        grid=(B, S // step),
        in_specs=specs,
        out_specs=out_specs,
        out_shape=out_shape,
        scratch_shapes=[pltpu.VMEM((W // qw, qw, qw), F32), pltpu.VMEM((8, C), F32)],
        compiler_params=_cparams("parallel", "arbitrary"),
        name="rwkv",
    )(*args)
    return out[0], (out[1] if v_res is None else v_first)


def _merge_kernel(x_ref, mod_ref, pg_ref, qg_ref, ya_ref, yb_ref, yc_ref, wg_ref,
                  wa_ref, wb_ref, wc_ref, ow_ref, o_ref):
    x = x_ref[...]
    D = x.shape[1]
    h = _pre(x, pg_ref[...], mod_ref).astype(BF16)
    merged = None
    for i, (y_ref, w_ref) in enumerate(((ya_ref, wa_ref), (yb_ref, wb_ref), (yc_ref, wc_ref))):
        gate = _sigmoid(jnp.dot(h, wg_ref[:, i * D:(i + 1) * D], preferred_element_type=F32))
        term = gate * jnp.dot(y_ref[...], w_ref[...], preferred_element_type=F32)
        merged = term if merged is None else merged + term
    y = _dot(merged, ow_ref[...])
    o_ref[...] = x + mod_ref[2:3, :] * _rms(y, qg_ref[...])


def _merge(x, mod3, pre_g, post_g, y_a, y_b, y_c, w_gate, branch_w, out_w):
    B, S, D = x.shape
    tm = min(FFN_TILE, S)
    tok = lambda n: pl.BlockSpec((None, tm, n), lambda b, i: (b, i, 0))
    w_a = jnp.pad(branch_w[0].reshape(NSA_HEADS, NSA_DK, D), ((0, 0), (0, NSA_VAUG - NSA_DK), (0, 0)))
    w_a = w_a.reshape(NSA_HEADS * NSA_VAUG, D)
    widths = (y_a.shape[-1], y_b.shape[-1], y_c.shape[-1])
    return pl.pallas_call(
        _merge_kernel,
        grid=(B, S // tm),
        in_specs=[tok(D), pl.BlockSpec((None, 3, D), lambda b, i: (b, 0, 0)),
                  _const_spec((1, D)), _const_spec((1, D))]
                 + [tok(n) for n in widths]
                 + [_const_spec((D, N_BRANCH * D))] + [_const_spec((n, D)) for n in widths]
                 + [_const_spec((D, D))],
        out_specs=tok(D),
        out_shape=jax.ShapeDtypeStruct((B, S, D), F32),
        compiler_params=_cparams("parallel", "parallel"),
        name="merge",
    )(x, mod3, pre_g.reshape(1, D), post_g.reshape(1, D), y_a, y_b, y_c, w_gate,
      w_a, branch_w[1], branch_w[2], out_w)


def kernel(x, c, ada_w, ada_b, pre_g, post_g, ffn_w_in, ffn_w_out, mix_w_in, branch_w, out_w,
           cmp_k_w1, cmp_k_w2, cmp_k_pe, cmp_v_w1, cmp_v_w2, cmp_v_pe,
           sgu_ln_g, sgu_ln_b, sgu_w, sgu_b,
           rwkv_mu, rwkv_w0, rwkv_w2, rwkv_a0, rwkv_a2, rwkv_g2, rwkv_kk, rwkv_ka, rwkv_rk,
           rwkv_lnx_g, rwkv_lnx_b, rwkv_v0, rwkv_v1, rwkv_v2):
    B, S, D = x.shape
    depth = ada_w.shape[0]
    mod = _ada_mod(c, ada_w, ada_b).reshape(depth, B, 3, 3, D)
    v_first = None
    for l in range(depth):
        x = _ffn(x, mod[l, :, 0], pre_g[l, 0], post_g[l, 0],
                 ffn_w_in[l, 0].astype(BF16), ffn_w_out[l, 0].astype(BF16))

        w = mix_w_in[l]
        n_in = NSA_COLS + 2 * SGU_WIDTH + RWKV_COLS
        q, kvc, vs, vw, gt, p_sgu, p_rwkv, ks, kw = _mix_proj(
            x, mod[l, :, 1], pre_g[l, 1], *_proj_weights(w))
        y_a = _nsa_mixer(q, kvc, vs, vw, gt, ks, kw, cmp_k_w1[l], cmp_k_w2[l], cmp_k_pe[l],
                         cmp_v_w1[l], cmp_v_w2[l], cmp_v_pe[l])
        y_b = _sgu_mixer(p_sgu, sgu_ln_g[l], sgu_ln_b[l], sgu_w[l], sgu_b[l])
        v_res = None if l == 0 else (rwkv_v0[l - 1], rwkv_v1[l - 1], rwkv_v2[l - 1])
        y_c, v_first = _rwkv_mixer(p_rwkv, rwkv_mu[l], rwkv_w0[l], rwkv_w2[l], rwkv_a0[l], rwkv_a2[l],
                                   rwkv_g2[l], rwkv_kk[l], rwkv_ka[l], rwkv_rk[l],
                                   rwkv_lnx_g[l], rwkv_lnx_b[l], v_first, v_res)
        x = _merge(x, mod[l, :, 1], pre_g[l, 1], post_g[l, 1], y_a, y_b, y_c,
                   w[:, n_in:].astype(BF16), branch_w[l].astype(BF16), out_w[l].astype(BF16))

        x = _ffn(x, mod[l, :, 2], pre_g[l, 2], post_g[l, 2],
                 ffn_w_in[l, 1].astype(BF16), ffn_w_out[l, 1].astype(BF16))
    return x
```

```python
import functools

import numpy as np
import jax
import jax.numpy as jnp
from jax import lax
from jax.experimental import pallas as pl
from jax.experimental.pallas import tpu as pltpu

D_MODEL = 1024
DEPTH = 2
NSA_HEADS = 8
NSA_GROUPS = 2
NSA_HPG = NSA_HEADS // NSA_GROUPS
NSA_DK = 64
CMP_LEN = 32
CMP_STRIDE = 16
CMP_HID = 256
SLC_BLOCK = 64
SLC_TOPN = 16
WINDOW = 512
Q_BLOCK = 128
SGU_CHUNK = 128
SGU_GROUPS = 4
SGU_WIDTH = 512
RWKV_HEADS = 8
RWKV_HEAD_DIM = 64
RWKV_WIDTH = RWKV_HEADS * RWKV_HEAD_DIM
DECAY_LORA = 64
AAA_LORA = 64
MV_LORA = 32
GATE_LORA = 128
N_BRANCH = 3
BRANCH_WIDTH = 512
D_FF = 2816
MACARON_WEIGHT = 0.5
RMS_EPS = 1e-6
LN_EPS = 1e-5
LNX_EPS = 64e-5
NEG_INF = -1e30
FORCE_SCORE = 1e4

NSA_COLS = NSA_HEADS * NSA_DK + 6 * NSA_GROUPS * NSA_DK + 3 * NSA_HEADS
RWKV_COLS = 3 * RWKV_WIDTH + DECAY_LORA + AAA_LORA + GATE_LORA
Q_COLS = NSA_HEADS * NSA_DK
KV_COLS = 6 * NSA_GROUPS * NSA_DK
GT_COLS = 3 * NSA_HEADS
LANES = 128
NSA_STEP = 256
NSA_VAUG = 128
NSA_AUG = 128
POS_SPLIT = 64
MASK_BIG = 2.0 ** 100

BF16 = jnp.bfloat16
F32 = jnp.float32

FFN_TILE = 512
FFN_CHUNK = 1408
SEL_TILE = 512
RWKV_CHUNK = 64
RWKV_PACK = 4
RWKV_STEP = 256
SGU_TILE = 512
VMEM_LIMIT = 56 * 1024 * 1024


def _cparams(*sem):
    return pltpu.CompilerParams(dimension_semantics=sem, vmem_limit_bytes=VMEM_LIMIT)


def _dot(a, b):
    return jnp.dot(a.astype(BF16), b.astype(BF16), preferred_element_type=F32)


def _dot_nt(a, b):
    return lax.dot_general(a.astype(BF16), b.astype(BF16), (((1,), (1,)), ((), ())),
                           preferred_element_type=F32)


def _dot_tn(a, b):
    return lax.dot_general(a.astype(BF16), b.astype(BF16), (((0,), (0,)), ((), ())),
                           preferred_element_type=F32)


def _split(x):
    hi = x.astype(BF16)
    lo = (x - hi.astype(F32)).astype(BF16)
    return hi, lo


def _dot_x2(a, b):
    hi, lo = _split(a)
    return _dot(hi, b) + _dot(lo, b)


def _sigmoid(x):
    return 1.0 / (1.0 + jnp.exp(-x))


def _silu(x):
    return x * _sigmoid(x)


def _rms(x, g):
    return x * lax.rsqrt(jnp.mean(x * x, axis=-1, keepdims=True) + RMS_EPS) * g


def _pre(x, g, mod_ref):
    return _rms(x, g) * (1.0 + mod_ref[1:2, :]) + mod_ref[0:1, :]


def _ada_kernel(c_ref, w_ref, b_ref, o_ref):
    cond = _silu(c_ref[...])
    o_ref[...] = _dot(cond, w_ref[...]) + b_ref[...]


def _ada_mod(c, ada_w, ada_b):
    L, D, N = ada_w.shape
    B = c.shape[0]
    tn = 1536
    return pl.pallas_call(
        _ada_kernel,
        grid=(L, N // tn),
        in_specs=[
            pl.BlockSpec((B, D), lambda l, j: (0, 0)),
            pl.BlockSpec((None, D, tn), lambda l, j: (l, 0, j)),
            pl.BlockSpec((None, 1, tn), lambda l, j: (l, 0, j)),
        ],
        out_specs=pl.BlockSpec((None, B, tn), lambda l, j: (l, 0, j)),
        out_shape=jax.ShapeDtypeStruct((L, B, N), F32),
        compiler_params=_cparams("arbitrary", "arbitrary"),
        name="ada_mod",
    )(c, ada_w, ada_b.reshape(L, 1, N))


def _ffn_kernel(x_ref, mod_ref, pg_ref, qg_ref, win_ref, wout_ref, o_ref):
    x = x_ref[...]
    h = _pre(x, pg_ref[...], mod_ref).astype(BF16)
    acc = None
    for c in range(D_FF // FFN_CHUNK):
        lo, hi = c * FFN_CHUNK, (c + 1) * FFN_CHUNK
        gate = jnp.dot(h, win_ref[:, lo:hi], preferred_element_type=F32)
        up = jnp.dot(h, win_ref[:, D_FF + lo:D_FF + hi], preferred_element_type=F32)
        act = (_silu(gate) * up).astype(BF16)
        part = jnp.dot(act, wout_ref[lo:hi, :], preferred_element_type=F32)
        acc = part if acc is None else acc + part
    o_ref[...] = x + MACARON_WEIGHT * mod_ref[2:3, :] * _rms(acc, qg_ref[...])


def _const_spec(shape):
    nd = len(shape)
    return pl.BlockSpec(shape, lambda *_: (0,) * nd, pipeline_mode=pl.Buffered(1))


def _ffn(x, mod3, pre_g, post_g, w_in, w_out):
    B, S, D = x.shape
    tm = min(FFN_TILE, S)
    return pl.pallas_call(
        _ffn_kernel,
        grid=(B, S // tm),
        in_specs=[
            pl.BlockSpec((None, tm, D), lambda b, i: (b, i, 0)),
            pl.BlockSpec((None, 3, D), lambda b, i: (b, 0, 0)),
            _const_spec((1, D)),
            _const_spec((1, D)),
            _const_spec((D, 2 * D_FF)),
            _const_spec((D_FF, D)),
        ],
        out_specs=pl.BlockSpec((None, tm, D), lambda b, i: (b, i, 0)),
        out_shape=jax.ShapeDtypeStruct((B, S, D), F32),
        compiler_params=_cparams("parallel", "parallel"),
        name="ffn",
    )(x, mod3, pre_g.reshape(1, D), post_g.reshape(1, D), w_in, w_out)


Q_AUG_COLS = NSA_HEADS * NSA_AUG
V_AUG_COLS = NSA_GROUPS * NSA_VAUG
CMP_COLS = 2 * NSA_GROUPS * NSA_DK
PROJ_WIDTHS = (Q_AUG_COLS, CMP_COLS, V_AUG_COLS, V_AUG_COLS, LANES, 2 * SGU_WIDTH, RWKV_COLS)
PROJ_COLS = sum(PROJ_WIDTHS)
KT_ROWS = 2 * NSA_GROUPS * NSA_AUG


def _proj_kernel(x_ref, mod_ref, pg_ref, w_ref, wkt_ref, qc_ref, vc_ref,
                 q_ref, cmp_ref, vs_ref, vw_ref, gt_ref, sgu_ref, rw_ref, ks_ref, kw_ref):
    h = _pre(x_ref[...], pg_ref[...], mod_ref).astype(BF16)
    tm = h.shape[0]
    edges = np.cumsum((0,) + PROJ_WIDTHS)
    cols = [jnp.dot(h, w_ref[:, int(a):int(b)], preferred_element_type=F32)
            for a, b in zip(edges[:-1], edges[1:])]
    q_ref[...] = (cols[0] * NSA_DK ** -0.5 + qc_ref[...]).astype(BF16)
    cmp_ref[...] = cols[1].astype(BF16)
    vs_ref[...] = (cols[2] + vc_ref[...]).astype(BF16)
    vw_ref[...] = (cols[3] + vc_ref[...]).astype(BF16)
    gt_ref[...] = cols[4]
    sgu_ref[...] = cols[5]
    rw_ref[...] = cols[6]
    kt = _dot_nt(wkt_ref[...], h)
    pos = pl.program_id(1) * tm + lax.broadcasted_iota(jnp.int32, (1, tm), 1)
    row = lax.broadcasted_iota(jnp.int32, (NSA_AUG, 1), 0)
    pos_rows = jnp.where(row == NSA_DK, (pos // POS_SPLIT).astype(F32),
                         jnp.where(row == NSA_DK + 1, (pos % POS_SPLIT).astype(F32), 0.0))
    for g in range(NSA_GROUPS):
        ks_ref[g, 0] = (kt[g * NSA_AUG:(g + 1) * NSA_AUG] + pos_rows).astype(BF16)
        kw = (kt[(NSA_GROUPS + g) * NSA_AUG:(NSA_GROUPS + g + 1) * NSA_AUG] + pos_rows).astype(BF16)
        for t in range(tm // LANES):
            kw_ref[g, t] = kw[:, t * LANES:(t + 1) * LANES]


def _proj_weights(w):
    D = w.shape[0]
    G, DK = NSA_GROUPS, NSA_DK
    q, kc, vc, ks, vs, kw, vw = [w[:, a:b] for a, b in zip(
        np.cumsum((0, Q_COLS) + (G * DK,) * 5), np.cumsum((Q_COLS,) + (G * DK,) * 6))]
    gt = w[:, Q_COLS + KV_COLS:NSA_COLS]

    def padded(z, width):
        n = z.shape[1] // DK
        return jnp.pad(z.reshape(D, n, DK), ((0, 0), (0, 0), (0, width - DK))).reshape(D, n * width)

    token_major = jnp.concatenate(
        [padded(q, NSA_AUG), kc, vc, padded(vs, NSA_VAUG), padded(vw, NSA_VAUG),
         jnp.pad(gt, ((0, 0), (0, LANES - GT_COLS))), w[:, NSA_COLS:NSA_COLS + 2 * SGU_WIDTH + RWKV_COLS]],
        axis=1).astype(BF16)
    keys_t = jnp.concatenate([padded(ks, NSA_AUG), padded(kw, NSA_AUG)], axis=1).T.astype(BF16)
    return token_major, keys_t


def _mix_proj(x, mod3, pre_g, w, wkt):
    B, S, D = x.shape
    tm = min(FFN_TILE, S)
    slopes = 2.0 ** -np.arange(1, NSA_HEADS + 1, dtype=np.float32)
    q_const = np.zeros((NSA_HEADS, NSA_AUG), np.float32)
    q_const[:, NSA_DK] = POS_SPLIT * slopes
    q_const[:, NSA_DK + 1] = slopes
    v_const = np.zeros((NSA_GROUPS, NSA_VAUG), np.float32)
    v_const[:, NSA_DK] = 1.0
    dtypes = (BF16, BF16, BF16, BF16, F32, F32, F32)
    tok = lambda n: pl.BlockSpec((None, tm, n), lambda b, i: (b, i, 0))
    return pl.pallas_call(
        _proj_kernel,
        grid=(B, S // tm),
        in_specs=[
            tok(D),
            pl.BlockSpec((None, 3, D), lambda b, i: (b, 0, 0)),
            _const_spec((1, D)),
            _const_spec((D, PROJ_COLS)),
            _const_spec((KT_ROWS, D)),
            _const_spec((1, Q_AUG_COLS)),
            _const_spec((1, V_AUG_COLS)),
        ],
        out_specs=[tok(n) for n in PROJ_WIDTHS] + [
            pl.BlockSpec((None, NSA_GROUPS, 1, NSA_AUG, tm), lambda b, i: (b, 0, i, 0, 0)),
            pl.BlockSpec((None, NSA_GROUPS, tm // LANES, NSA_AUG, LANES), lambda b, i: (b, 0, i, 0, 0))],
        out_shape=[jax.ShapeDtypeStruct((B, S, n), dt) for n, dt in zip(PROJ_WIDTHS, dtypes)] + [
            jax.ShapeDtypeStruct((B, NSA_GROUPS, S // tm, NSA_AUG, tm), BF16),
            jax.ShapeDtypeStruct((B, NSA_GROUPS, S // LANES, NSA_AUG, LANES), BF16)],
        compiler_params=_cparams("parallel", "parallel"),
        name="mix_proj",
    )(x, mod3, pre_g.reshape(1, D), w, wkt,
      jnp.asarray(q_const.reshape(1, -1)), jnp.asarray(v_const.reshape(1, -1)))


def _cmp_kernel(xk_ref, xv_ref, pek_ref, pev_ref, w1k_ref, w1v_ref, w2kt_ref, w2v_ref,
                kct_ref, vc_ref):
    half = (CMP_LEN // 2) * NSA_DK

    def hidden(x_ref, pe_ref, w1_ref):
        x = x_ref[...].astype(F32)
        n = x.shape[0]
        top = _dot(x + pe_ref[0:1, :], w1_ref[0:half, :])
        bot = _dot(x + pe_ref[1:2, :], w1_ref[half:2 * half, :])
        return _silu(top + pltpu.roll(bot, n - 1, axis=0))

    kct_ref[...] = _dot_nt(w2kt_ref[...], hidden(xk_ref, pek_ref, w1k_ref)).astype(BF16)
    vc_ref[...] = _dot(hidden(xv_ref, pev_ref, w1v_ref), w2v_ref[...]).astype(BF16)


def _compress(xk, xv, pe_k, pe_v, w1k, w1v, w2kt, w2v):
    B, G, NC, F = xk.shape
    blk = lambda *s: pl.BlockSpec((None, None) + s, lambda b, g: (b, g, 0, 0))
    return pl.pallas_call(
        _cmp_kernel,
        grid=(B, G),
        in_specs=[blk(NC, F), blk(NC, F), _const_spec((2, F)), _const_spec((2, F)),
                  _const_spec((2 * F, CMP_HID)), _const_spec((2 * F, CMP_HID)),
                  _const_spec((NSA_DK, CMP_HID)), _const_spec((CMP_HID, NSA_VAUG))],
        out_specs=[blk(NSA_DK, NC), blk(NC, NSA_VAUG)],
        out_shape=[jax.ShapeDtypeStruct((B, G, NSA_DK, NC), BF16),
                   jax.ShapeDtypeStruct((B, G, NC, NSA_VAUG), BF16)],
        compiler_params=_cparams("parallel", "parallel"),
        name="nsa_compress",
    )(xk, xv, pe_k, pe_v, w1k, w1v, w2kt, w2v)


def _masked_softmax(s, valid):
    s = jnp.where(valid, s, NEG_INF)
    m = jnp.max(s, axis=-1, keepdims=True)
    p = jnp.where(valid, jnp.exp(s - m), 0.0)
    l = jnp.sum(p, axis=-1, keepdims=True)
    return p * (1.0 / jnp.where(l > 0.0, l, 1.0))


def _nsa_kernel(q_ref, gt_ref, kc_ref, vc_ref, ks_ref, vs_ref, kw_ref, vw_ref, selt_ref, e_ref,
                o_ref, imp_ref, tiles_ref, *, n_top, tk):
    s0 = pl.program_id(2) * NSA_STEP
    R = NSA_HPG * Q_BLOCK
    n_slc = selt_ref.shape[0]
    n_cmp = kc_ref.shape[-1]
    parts = range(NSA_STEP // Q_BLOCK)
    each = lambda fn, *lists: [fn(*xs) for xs in zip(*lists)]
    rows = lambda c: slice(c * Q_BLOCK, (c + 1) * Q_BLOCK)
    head_cols = lambda h: slice(h * NSA_AUG, (h + 1) * NSA_AUG)
    q = [jnp.concatenate([q_ref[rows(c), head_cols(h)] for h in range(NSA_HPG)], axis=0) for c in parts]
    q_row = lax.broadcasted_iota(jnp.int32, (R, 1), 0) & (Q_BLOCK - 1)
    t_i = [s0 + c * Q_BLOCK + q_row for c in parts]

    def normalised(acc):
        return acc * (1.0 / acc[:, NSA_DK:NSA_DK + 1])

    cmp_end = lax.broadcasted_iota(jnp.int32, (1, n_cmp), 1) * CMP_STRIDE + (CMP_LEN - 1)
    s_c = [jnp.dot(x, kc_ref[...], preferred_element_type=F32) for x in q]
    p_c = each(lambda s, t: _masked_softmax(s, cmp_end <= t), s_c, t_i)
    o_c = [_dot(p, vc_ref[...]) for p in p_c]

    j = lax.broadcasted_iota(jnp.int32, (n_slc, Q_BLOCK), 0)
    imp = []
    for c in parts:
        p_sum = p_c[c][0:Q_BLOCK]
        for h in range(1, NSA_HPG):
            p_sum = p_sum + p_c[c][h * Q_BLOCK:(h + 1) * Q_BLOCK]
        hi, lo = _split(p_sum)
        raw = _dot_nt(selt_ref[...], hi) + _dot_nt(selt_ref[...], lo)
        tq = s0 + c * Q_BLOCK + lax.broadcasted_iota(jnp.int32, (n_slc, Q_BLOCK), 1)
        cur = tq // SLC_BLOCK
        forced = (j == 0) | (j == cur) | (j == cur - 1)
        live = j * SLC_BLOCK <= tq
        imp.append(jnp.where(forced, FORCE_SCORE, jnp.where(live, raw, NEG_INF)))
        imp_ref[c] = imp[c]

    def rank_body(i, cnt):
        tie = jnp.where(j > i, 1.0, 0.0)
        out = []
        for c in parts:
            r = imp_ref[c, pl.ds(i, 1), :]
            out.append(cnt[c] + jnp.where(r > imp[c], 1.0, jnp.where(r == imp[c], tie, 0.0)))
        return tuple(out)

    n_live = (s0 + NSA_STEP) // SLC_BLOCK
    cnt = lax.fori_loop(0, n_live, rank_body, tuple(jnp.zeros((n_slc, Q_BLOCK), F32) for _ in parts))
    lhs = []
    for c in parts:
        dropped = jnp.concatenate(
            [jnp.where(cnt[c] < n_top, 0.0, 1.0), jnp.zeros((NSA_AUG - n_slc, Q_BLOCK), F32)], axis=0)
        dropped = dropped.T.astype(BF16)
        lhs.append(jnp.concatenate([q[c], jnp.concatenate([dropped] * NSA_HPG, axis=0)], axis=1))

    last = (s0 + NSA_STEP - 1) // tk
    chosen = jnp.where(cnt[0] < n_top, 1.0, 0.0)
    for c in parts[1:]:
        chosen = jnp.maximum(chosen, jnp.where(cnt[c] < n_top, 1.0, 0.0))
    blocks_per_tile = tk // SLC_BLOCK
    n_used = jnp.int32(0)
    for kt in range(n_slc // blocks_per_tile):
        used = jnp.max(chosen[kt * blocks_per_tile:(kt + 1) * blocks_per_tile, :]) > 0.0
        tiles_ref[n_used] = kt
        n_used = n_used + jnp.where(used & (kt < last), 1, 0).astype(jnp.int32)

    def sel_tile(kt, carry, diagonal):
        m, acc = carry[0::2], carry[1::2]
        rhs = jnp.concatenate([ks_ref[kt], e_ref[kt]], axis=0)
        v = vs_ref[pl.ds(pl.multiple_of(kt * tk, tk), tk), :]
        s = [jnp.dot(x, rhs, preferred_element_type=F32) for x in lhs]
        if diagonal:
            pos = kt * tk + lax.broadcasted_iota(jnp.int32, (1, tk), 1)
            s = each(lambda x, t: jnp.where(pos <= t, x, NEG_INF), s, t_i)
        m_new = each(lambda x, y: jnp.maximum(x, jnp.max(y, axis=-1, keepdims=True)), m, s)
        alpha = each(lambda x, y: jnp.exp(x - y), m, m_new)
        p = each(lambda x, y: jnp.exp(x - y).astype(BF16), s, m_new)
        acc = each(lambda a, x, y: a * x + jnp.dot(y, v, preferred_element_type=F32), alpha, acc, p)
        out = []
        for x, y in zip(m_new, acc):
            out += [x, y]
        return tuple(out)

    init = []
    for _ in parts:
        init += [jnp.full((R, 1), NEG_INF, F32), jnp.zeros((R, NSA_VAUG), F32)]
    carry = lax.fori_loop(0, n_used, lambda i, x: sel_tile(tiles_ref[i], x, False), tuple(init))
    o_s = [normalised(x) for x in sel_tile(last, carry, True)[1::2]]

    n_wt = (WINDOW + Q_BLOCK) // LANES
    lane_pos = lax.broadcasted_iota(jnp.int32, (1, LANES), 1)
    s_w, start = [], []
    for c in parts:
        start.append(jnp.maximum(s0 + c * Q_BLOCK - WINDOW, 0))
        tiles = []
        for i in range(n_wt):
            pos = start[c] + i * LANES + lane_pos
            x = jnp.dot(q[c], kw_ref[start[c] // LANES + i], preferred_element_type=F32)
            x = jnp.where(pos <= t_i[c], x, NEG_INF)
            if i == 0:
                x = jnp.where(pos > t_i[c] - WINDOW, x, NEG_INF)
            tiles.append(x)
        s_w.append(jnp.concatenate(tiles, axis=1))
    p_w = [jnp.exp(x - jnp.max(x, axis=-1, keepdims=True)) for x in s_w]
    o_w = [normalised(_dot(p, vw_ref[pl.ds(pl.multiple_of(st, LANES), n_wt * LANES), :]))
           for p, st in zip(p_w, start)]

    for c in parts:
        gate = _sigmoid(gt_ref[:, rows(c), :].reshape(R, 3))
        o = gate[:, 0:1] * o_c[c] + gate[:, 1:2] * o_s[c] + gate[:, 2:3] * o_w[c]
        for h in range(NSA_HPG):
            o_ref[rows(c), head_cols(h)] = o[h * Q_BLOCK:(h + 1) * Q_BLOCK].astype(o_ref.dtype)


def _selection_map_t(n_cmp_rows, n_slc):
    rs, rc = SLC_BLOCK // CMP_STRIDE, CMP_LEN // CMP_STRIDE
    m = np.zeros((n_slc, n_cmp_rows), np.float32)
    for jj in range(n_slc):
        for a in range(rs):
            for b in range(rc):
                i = rs * jj - a - b
                if 0 <= i < n_cmp_rows - 1:
                    m[jj, i] += 1.0
    return jnp.asarray(m, BF16)


def _nsa_attention(q, gt, kc, vc, ks, vs, kw, vw):
    B, S, _ = q.shape
    G, A = NSA_GROUPS, NSA_AUG
    n_slc = S // SLC_BLOCK
    n_cmp = kc.shape[-1]
    n_kt, _, tk = ks.shape[2:]
    selt = _selection_map_t(n_cmp, n_slc)
    key_blk = np.arange(S).reshape(n_kt, 1, tk) // SLC_BLOCK
    e = np.where(np.arange(A).reshape(1, A, 1) == key_blk, -MASK_BIG, 0.0)
    heads = pl.BlockSpec((None, NSA_STEP, NSA_HPG * A), lambda b, g, i: (b, i, g))
    values = pl.BlockSpec((None, S, NSA_VAUG), lambda b, g, i: (b, 0, g))
    per_g = lambda *s: pl.BlockSpec((None, None) + s, lambda b, g, i: (b, g) + (0,) * len(s))
    kern = functools.partial(_nsa_kernel, n_top=min(SLC_TOPN, n_slc), tk=tk)
    return pl.pallas_call(
        kern,
        grid=(B, G, S // NSA_STEP),
        in_specs=[heads, pl.BlockSpec((None, NSA_HPG, NSA_STEP, 3), lambda b, g, i: (b, g, i, 0)),
                  per_g(A, n_cmp), per_g(n_cmp, NSA_VAUG),
                  per_g(n_kt, A, tk), values, per_g(S // LANES, A, LANES), values,
                  pl.BlockSpec((n_slc, n_cmp), lambda b, g, i: (0, 0)),
                  pl.BlockSpec((n_kt, A, tk), lambda b, g, i: (0, 0, 0))],
        out_specs=heads,
        out_shape=jax.ShapeDtypeStruct((B, S, NSA_HEADS * NSA_VAUG), BF16),
        scratch_shapes=[pltpu.VMEM((NSA_STEP // Q_BLOCK, n_slc, Q_BLOCK), F32),
                        pltpu.SMEM((n_kt,), jnp.int32)],
        compiler_params=_cparams("parallel", "parallel", "arbitrary"),
        name="nsa_attention",
    )(q, gt, kc, vc, ks, vs, kw, vw, selt, jnp.asarray(e, BF16))


def _with_position_rows(kt, pos):
    lead = kt.shape[:-2]
    n = kt.shape[-1]
    hi = jnp.broadcast_to(jnp.asarray(pos // POS_SPLIT, BF16)[..., None, :], lead + (1, n))
    lo = jnp.broadcast_to(jnp.asarray(pos % POS_SPLIT, BF16)[..., None, :], lead + (1, n))
    pad = jnp.zeros(lead + (NSA_AUG - NSA_DK - 2, n), BF16)
    return jnp.concatenate([kt, hi, lo, pad], axis=-2)


def _nsa_mixer(q, kvc, vs, vw, gt, ks, kw, k_w1, k_w2, k_pe, v_w1, v_w2, v_pe):
    B, S, _ = q.shape
    G, DK = NSA_GROUPS, NSA_DK
    half = CMP_LEN // 2
    nc = S // half

    def half_blocks(z):
        z = z.reshape(B, nc, half, G, DK)
        return z.transpose(0, 3, 1, 2, 4).reshape(B, G, nc, half * DK)

    kct, vcc = _compress(
        half_blocks(kvc[:, :, :G * DK]), half_blocks(kvc[:, :, G * DK:]),
        k_pe.reshape(2, half * DK), v_pe.reshape(2, half * DK),
        k_w1.astype(BF16), v_w1.astype(BF16), k_w2.T.astype(BF16),
        jnp.pad(v_w2, ((0, 0), (0, NSA_VAUG - DK))).astype(BF16))
    gth = gt[:, :, :GT_COLS].reshape(B, S, NSA_HEADS, 3).transpose(0, 2, 1, 3)
    kca = _with_position_rows(kct, np.arange(nc) * CMP_STRIDE + (CMP_LEN - 1))
    return _nsa_attention(q, gth, kca, vcc, ks, vs, kw, vw)


def _gelu_tanh(x):
    return 0.5 * x * (1.0 + jnp.tanh(np.sqrt(2.0 / np.pi).astype(np.float32) * (x + 0.044715 * (x * x * x))))


def _sgu_kernel(p_ref, g_ref, b_ref, w_ref, bs_ref, o_ref):
    ts = p_ref.shape[0]
    gd = SGU_WIDTH // SGU_GROUPS
    u = _gelu_tanh(p_ref[:, 0:SGU_WIDTH])
    v = _gelu_tanh(p_ref[:, SGU_WIDTH:2 * SGU_WIDTH])
    mu = jnp.mean(v, axis=-1, keepdims=True)
    vc = v - mu
    var = jnp.mean(vc * vc, axis=-1, keepdims=True)
    vn = (vc * lax.rsqrt(var + LN_EPS) * g_ref[...] + b_ref[...]).astype(BF16)
    causal = (lax.broadcasted_iota(jnp.int32, (SGU_CHUNK, SGU_CHUNK), 0)
              >= lax.broadcasted_iota(jnp.int32, (SGU_CHUNK, SGU_CHUNK), 1))
    ws = [jnp.where(causal, w_ref[i], 0.0).astype(BF16) for i in range(SGU_GROUPS)]
    for n in range(ts // SGU_CHUNK):
        r0, r1 = n * SGU_CHUNK, (n + 1) * SGU_CHUNK
        s = jnp.concatenate(
            [jnp.dot(ws[i], vn[r0:r1, i * gd:(i + 1) * gd], preferred_element_type=F32)
             for i in range(SGU_GROUPS)], axis=1) + bs_ref[...]
        o_ref[r0:r1, :] = (u[r0:r1] * s).astype(o_ref.dtype)


def _sgu_mixer(p, ln_g, ln_b, w_s, b_s):
    B, S, _ = p.shape
    ts = min(SGU_TILE, S)
    gd = SGU_WIDTH // SGU_GROUPS
    bias = jnp.repeat(b_s.T, gd, axis=1)
    return pl.pallas_call(
        _sgu_kernel,
        grid=(B, S // ts),
        in_specs=[pl.BlockSpec((None, ts, 2 * SGU_WIDTH), lambda b, i: (b, i, 0)),
                  _const_spec((1, SGU_WIDTH)), _const_spec((1, SGU_WIDTH)),
                  _const_spec((SGU_GROUPS, SGU_CHUNK, SGU_CHUNK)), _const_spec((SGU_CHUNK, SGU_WIDTH))],
        out_specs=pl.BlockSpec((None, ts, SGU_WIDTH), lambda b, i: (b, i, 0)),
        out_shape=jax.ShapeDtypeStruct((B, S, SGU_WIDTH), BF16),
        compiler_params=_cparams("parallel", "parallel"),
        name="sgu",
    )(p, ln_g.reshape(1, -1), ln_b.reshape(1, -1), w_s, bias)


def _head_sum(x, bd_ref):
    return _dot_x2(x, bd_ref[...])


def _rwkv_operands(p_ref, carry, mu_ref, w0_ref, w2_ref, a0_ref, a2_ref, g2_ref, kk_ref, ka_ref, bd_ref,
                   v_res_refs):
    W = RWKV_WIDTH
    p = p_ref[...]
    tr = p.shape[0]

    @pl.when(pl.program_id(1) == 0)
    def _():
        carry[...] = jnp.zeros_like(carry)

    rowi = lax.broadcasted_iota(jnp.int32, (tr, 1), 0)
    p_prev = jnp.where(rowi == 0, carry[0:1, :], pltpu.roll(p, 1, axis=0))
    carry[0:1, :] = p[tr - 1:tr, :]
    ps = p + (p_prev - p) * mu_ref[...]
    r, k, v = ps[:, 0:W], ps[:, W:2 * W], ps[:, 2 * W:3 * W]
    wa = ps[:, 3 * W:3 * W + DECAY_LORA + AAA_LORA]
    gd = ps[:, 3 * W + DECAY_LORA + AAA_LORA:]
    w = w0_ref[...] + _dot(jnp.tanh(wa), w2_ref[...])
    x = -w
    softplus = jnp.maximum(x, 0.0) + jnp.log(1.0 + jnp.exp(-jnp.abs(x)))
    ld = -jnp.exp(-softplus - 0.5)
    a = _sigmoid(a0_ref[...] + _dot(wa, a2_ref[...]))
    g = _dot(_sigmoid(gd), g2_ref[...])
    if v_res_refs is not None:
        vf_ref, v0_ref, v1_ref, v2_ref = v_res_refs
        lora = _dot(_dot(v, v1_ref[...]), v2_ref[...])
        v = v + (vf_ref[...] - v) * _sigmoid(v0_ref[...] + lora)
    kk = k * kk_ref[...]
    norm = jnp.sqrt(_head_sum(kk * kk, bd_ref))
    kn = kk * (1.0 / jnp.maximum(norm, 1e-12))
    return r, ld, k * (1.0 + (a - 1.0) * ka_ref[...]), v, kn, kn * a, g


def _head_block_diag():
    idx = np.arange(RWKV_WIDTH) // RWKV_HEAD_DIM
    return jnp.asarray(idx[:, None] == idx[None, :], BF16)


def _rwkv_kernel(*refs, has_vres):
    (p_ref, mu_ref, w0_ref, w2_ref, a0_ref, a2_ref, g2_ref, kk_ref, ka_ref,
     rk_ref, lg_ref, lb_ref, bd_ref) = refs[:13]
    if has_vres:
        v_res_refs = refs[13:17]
        o_ref, state, carry = refs[17:]
    else:
        v_res_refs = None
        o_ref, v_ref, state, carry = refs[13:]
    r, ld, k, v, kn, ba, g = _rwkv_operands(p_ref, carry, mu_ref, w0_ref, w2_ref, a0_ref, a2_ref, g2_ref,
                                            kk_ref, ka_ref, bd_ref, v_res_refs)
    if not has_vres:
        v_ref[...] = v
    L = RWKV_CHUNK
    N = RWKV_HEAD_DIM
    NH = RWKV_PACK
    QW = NH * N
    n_sub = r.shape[0] // L

    @pl.when(pl.program_id(1) == 0)
    def _():
        state[...] = jnp.zeros_like(state)

    tt = lax.broadcasted_iota(jnp.int32, (n_sub * L, n_sub * L), 0)
    ts = lax.broadcasted_iota(jnp.int32, (n_sub * L, n_sub * L), 1)
    tril_ones = jnp.where((tt >= ts) & (tt // L == ts // L), 1.0, 0.0).astype(BF16)
    own = (lax.broadcasted_iota(jnp.int32, (QW, QW), 0) // N
           == lax.broadcasted_iota(jnp.int32, (QW, QW), 1) // N)
    ti = lax.broadcasted_iota(jnp.int32, (L, QW), 0)
    si = lax.broadcasted_iota(jnp.int32, (L, QW), 1) % L
    strict = ti > si
    incl = ti >= si
    eye_tok = jnp.where(ti == si, 1.0, 0.0)
    eye_ch = jnp.where(lax.broadcasted_iota(jnp.int32, (QW, QW), 0)
                       == lax.broadcasted_iota(jnp.int32, (QW, QW), 1), 1.0, 0.0)

    def stacked(x):
        x = x.astype(BF16)
        return jnp.where(own, jnp.concatenate([x] * NH, axis=0), jnp.zeros((), BF16))

    h1 = ld.astype(BF16)
    r1 = ld - h1.astype(F32)
    h2 = r1.astype(BF16)
    h3 = (r1 - h2.astype(F32)).astype(BF16)
    cum = _dot(tril_ones, h1) + (_dot(tril_ones, h2) + _dot(tril_ones, h3))
    w_in = jnp.exp(cum)
    w_inv = jnp.exp(-cum)
    rt = r * w_in
    kt = k * w_inv
    at = -kn * jnp.exp(cum - ld)
    bt = ba * w_inv

    n_grp = RWKV_WIDTH // QW
    chains = [(c, q) for c in range(n_sub) for q in range(n_grp)]
    each = lambda fn, *lists: [fn(*xs) for xs in zip(*lists)]
    pieces = lambda z: [z[c * L:(c + 1) * L, q * QW:(q + 1) * QW] for c, q in chains]
    cat0 = lambda *xs: jnp.concatenate(xs, axis=0)
    cat1 = lambda *xs: jnp.concatenate(xs, axis=1)
    a_t, r_t, b_t, k_t, v_t = pieces(at), pieces(rt), pieces(bt), pieces(kt), pieces(v)
    v_st = [stacked(x) for x in v_t]
    gram = each(lambda a, r, b, k: _dot_nt(cat0(a, r), cat0(stacked(b), stacked(k))), a_t, r_t, b_t, k_t)
    a_ab = [jnp.where(strict, x[0:L, 0:QW], 0.0) for x in gram]
    a_ak = [jnp.where(strict, x[0:L, QW:2 * QW], 0.0) for x in gram]
    m_rb = [jnp.where(incl, x[L:2 * L, 0:QW], 0.0) for x in gram]
    m_rk = [jnp.where(incl, x[L:2 * L, QW:2 * QW], 0.0) for x in gram]
    t_inv = [eye_tok + x for x in a_ab]
    power = each(lambda p: _dot(p, stacked(p)), a_ab)
    for _ in range(int(np.log2(L)) - 2):
        both = each(lambda p, t: _dot(cat0(p, t), stacked(p)), power, t_inv)
        power = [x[0:L] for x in both]
        t_inv = each(lambda t, x: t + x[L:2 * L], t_inv, both)
    t_inv = each(lambda t, p: t + _dot(t, stacked(p)), t_inv, power)
    akv = each(_dot, a_ak, v_st)
    au = each(lambda t, a, x: _dot(t, cat1(stacked(a), stacked(x))), t_inv, a_t, akv)
    ahat = [x[:, 0:QW] for x in au]
    uhat = [x[:, QW:2 * QW] for x in au]
    rhat = each(lambda r, m, a: r + _dot(m, stacked(a)), r_t, m_rb, ahat)
    y0 = each(lambda mb, mk, u, x: _dot(cat1(mb, mk), cat0(stacked(u), x)), m_rb, m_rk, uhat, v_st)
    w_last = [w_in[(c + 1) * L - 1:(c + 1) * L, q * QW:(q + 1) * QW] for c, q in chains]
    p_mat = each(lambda a, b, w: jnp.where(own, eye_ch + _dot_tn(a, b), 0.0) * w, ahat, b_t, w_last)
    q_mat = each(lambda u, x, b, k, w: jnp.where(own, _dot_tn(cat0(u, x), cat0(b, k)), 0.0) * w,
                 uhat, v_t, b_t, k_t, w_last)

    y_chunks = []
    for c in range(n_sub):
        idx = range(c * n_grp, (c + 1) * n_grp)
        s_old = [state[q] for q in range(n_grp)]
        y_chunks.append(jnp.concatenate([_dot_nt(rhat[i], s) + y0[i] for i, s in zip(idx, s_old)], axis=1))
        for q, i in enumerate(idx):
            state[q] = _dot(s_old[q], p_mat[i]) + q_mat[i]
    y = jnp.concatenate(y_chunks, axis=0)
    inv_n = 1.0 / N
    mean = _head_sum(y, bd_ref) * inv_n
    yc = y - mean
    var = _head_sum(yc * yc, bd_ref) * inv_n
    yn = yc * lax.rsqrt(var + LNX_EPS) * lg_ref[...] + lb_ref[...]
    bonus = _head_sum(r * k * rk_ref[...], bd_ref) * v
    o_ref[...] = ((yn + bonus) * g).astype(o_ref.dtype)


def _rwkv_mixer(p, mu, w0, w2, a0, a2, g2, k_k, k_a, r_k, lnx_g, lnx_b, v_first, v_res):
    B, S, C = p.shape
    W = RWKV_WIDTH
    step = min(RWKV_STEP, S)
    qw = RWKV_PACK * RWKV_HEAD_DIM
    row = lambda z: z.reshape(1, -1)
    lora_rows = DECAY_LORA + AAA_LORA
    w2p = jnp.zeros((lora_rows, W), F32).at[:DECAY_LORA].set(w2).astype(BF16)
    a2p = jnp.zeros((lora_rows, W), F32).at[DECAY_LORA:].set(a2).astype(BF16)
    tok = lambda n: pl.BlockSpec((None, step, n), lambda b, i: (b, i, 0))
    vec = _const_spec((1, W))
    args = [p, row(mu), row(w0), w2p, row(a0), a2p, g2.astype(BF16), row(k_k), row(k_a),
            row(r_k), row(lnx_g), row(lnx_b), _head_block_diag()]
    specs = [tok(C), _const_spec((1, C)), vec, _const_spec((lora_rows, W)), vec,
             _const_spec((lora_rows, W)), _const_spec((GATE_LORA, W)), vec, vec,
             vec, vec, vec, _const_spec((W, W))]
    out_specs = [tok(W)]
    out_shape = [jax.ShapeDtypeStruct((B, S, W), BF16)]
    if v_res is None:
        out_specs.append(tok(W))
        out_shape.append(jax.ShapeDtypeStruct((B, S, W), F32))
    else:
        v0, v1, v2 = v_res
        args += [v_first, row(v0), v1.astype(BF16), v2.astype(BF16)]
        specs += [tok(W), vec, _const_spec((W, MV_LORA)), _const_spec((MV_LORA, W))]
    out = pl.pallas_call(
        functools.partial(_rwkv_kernel, has_vres=v_res is not None),
        grid=(B, S // step),
        in_specs=specs,
        out_specs=out_specs,
        out_shape=out_shape,
        scratch_shapes=[pltpu.VMEM((W // qw, qw, qw), F32), pltpu.VMEM((8, C), F32)],
        compiler_params=_cparams("parallel", "arbitrary"),
        name="rwkv",
    )(*args)
    return out[0], (out[1] if v_res is None else v_first)


def _merge_kernel(x_ref, mod_ref, pg_ref, qg_ref, ya_ref, yb_ref, yc_ref, wg_ref,
                  wa_ref, wb_ref, wc_ref, ow_ref, o_ref):
    x = x_ref[...]
    D = x.shape[1]
    h = _pre(x, pg_ref[...], mod_ref).astype(BF16)
    merged = None
    for i, (y_ref, w_ref) in enumerate(((ya_ref, wa_ref), (yb_ref, wb_ref), (yc_ref, wc_ref))):
        gate = _sigmoid(jnp.dot(h, wg_ref[:, i * D:(i + 1) * D], preferred_element_type=F32))
        term = gate * jnp.dot(y_ref[...], w_ref[...], preferred_element_type=F32)
        merged = term if merged is None else merged + term
    y = _dot(merged, ow_ref[...])
    o_ref[...] = x + mod_ref[2:3, :] * _rms(y, qg_ref[...])


def _merge(x, mod3, pre_g, post_g, y_a, y_b, y_c, w_gate, branch_w, out_w):
    B, S, D = x.shape
    tm = min(FFN_TILE, S)
    tok = lambda n: pl.BlockSpec((None, tm, n), lambda b, i: (b, i, 0))
    w_a = jnp.pad(branch_w[0].reshape(NSA_HEADS, NSA_DK, D), ((0, 0), (0, NSA_VAUG - NSA_DK), (0, 0)))
    w_a = w_a.reshape(NSA_HEADS * NSA_VAUG, D)
    widths = (y_a.shape[-1], y_b.shape[-1], y_c.shape[-1])
    return pl.pallas_call(
        _merge_kernel,
        grid=(B, S // tm),
        in_specs=[tok(D), pl.BlockSpec((None, 3, D), lambda b, i: (b, 0, 0)),
                  _const_spec((1, D)), _const_spec((1, D))]
                 + [tok(n) for n in widths]
                 + [_const_spec((D, N_BRANCH * D))] + [_const_spec((n, D)) for n in widths]
                 + [_const_spec((D, D))],
        out_specs=tok(D),
        out_shape=jax.ShapeDtypeStruct((B, S, D), F32),
        compiler_params=_cparams("parallel", "parallel"),
        name="merge",
    )(x, mod3, pre_g.reshape(1, D), post_g.reshape(1, D), y_a, y_b, y_c, w_gate,
      w_a, branch_w[1], branch_w[2], out_w)


def kernel(x, c, ada_w, ada_b, pre_g, post_g, ffn_w_in, ffn_w_out, mix_w_in, branch_w, out_w,
           cmp_k_w1, cmp_k_w2, cmp_k_pe, cmp_v_w1, cmp_v_w2, cmp_v_pe,
           sgu_ln_g, sgu_ln_b, sgu_w, sgu_b,
           rwkv_mu, rwkv_w0, rwkv_w2, rwkv_a0, rwkv_a2, rwkv_g2, rwkv_kk, rwkv_ka, rwkv_rk,
           rwkv_lnx_g, rwkv_lnx_b, rwkv_v0, rwkv_v1, rwkv_v2):
    B, S, D = x.shape
    depth = ada_w.shape[0]
    mod = _ada_mod(c, ada_w, ada_b).reshape(depth, B, 3, 3, D)
    v_first = None
    for l in range(depth):
        x = _ffn(x, mod[l, :, 0], pre_g[l, 0], post_g[l, 0],
                 ffn_w_in[l, 0].astype(BF16), ffn_w_out[l, 0].astype(BF16))

        w = mix_w_in[l]
        n_in = NSA_COLS + 2 * SGU_WIDTH + RWKV_COLS
        q, kvc, vs, vw, gt, p_sgu, p_rwkv, ks, kw = _mix_proj(
            x, mod[l, :, 1], pre_g[l, 1], *_proj_weights(w))
        y_a = _nsa_mixer(q, kvc, vs, vw, gt, ks, kw, cmp_k_w1[l], cmp_k_w2[l], cmp_k_pe[l],
                         cmp_v_w1[l], cmp_v_w2[l], cmp_v_pe[l])
        y_b = _sgu_mixer(p_sgu, sgu_ln_g[l], sgu_ln_b[l], sgu_w[l], sgu_b[l])
        v_res = None if l == 0 else (rwkv_v0[l - 1], rwkv_v1[l - 1], rwkv_v2[l - 1])
        y_c, v_first = _rwkv_mixer(p_rwkv, rwkv_mu[l], rwkv_w0[l], rwkv_w2[l], rwkv_a0[l], rwkv_a2[l],
                                   rwkv_g2[l], rwkv_kk[l], rwkv_ka[l], rwkv_rk[l],
                                   rwkv_lnx_g[l], rwkv_lnx_b[l], v_first, v_res)
        x = _merge(x, mod[l, :, 1], pre_g[l, 1], post_g[l, 1], y_a, y_b, y_c,
                   w[:, n_in:].astype(BF16), branch_w[l].astype(BF16), out_w[l].astype(BF16))

        x = _ffn(x, mod[l, :, 2], pre_g[l, 2], post_g[l, 2],
                 ffn_w_in[l, 1].astype(BF16), ffn_w_out[l, 1].astype(BF16))
    return x
```

```python
import functools

import numpy as np
import jax
import jax.numpy as jnp
from jax import lax
from jax.experimental import pallas as pl
from jax.experimental.pallas import tpu as pltpu

D_MODEL = 1024
DEPTH = 2
NSA_HEADS = 8
NSA_GROUPS = 2
NSA_HPG = NSA_HEADS // NSA_GROUPS
NSA_DK = 64
CMP_LEN = 32
CMP_STRIDE = 16
CMP_HID = 256
SLC_BLOCK = 64
SLC_TOPN = 16
WINDOW = 512
Q_BLOCK = 128
SGU_CHUNK = 128
SGU_GROUPS = 4
SGU_WIDTH = 512
RWKV_HEADS = 8
RWKV_HEAD_DIM = 64
RWKV_WIDTH = RWKV_HEADS * RWKV_HEAD_DIM
DECAY_LORA = 64
AAA_LORA = 64
MV_LORA = 32
GATE_LORA = 128
N_BRANCH = 3
BRANCH_WIDTH = 512
D_FF = 2816
MACARON_WEIGHT = 0.5
RMS_EPS = 1e-6
LN_EPS = 1e-5
LNX_EPS = 64e-5
NEG_INF = -1e30
FORCE_SCORE = 1e4

NSA_COLS = NSA_HEADS * NSA_DK + 6 * NSA_GROUPS * NSA_DK + 3 * NSA_HEADS
RWKV_COLS = 3 * RWKV_WIDTH + DECAY_LORA + AAA_LORA + GATE_LORA
Q_COLS = NSA_HEADS * NSA_DK
KV_COLS = 6 * NSA_GROUPS * NSA_DK
GT_COLS = 3 * NSA_HEADS
LANES = 128
NSA_STEP = 256
NSA_VAUG = 128
NSA_AUG = 128
POS_SPLIT = 64
MASK_BIG = 2.0 ** 100

BF16 = jnp.bfloat16
F32 = jnp.float32

FFN_TILE = 512
FFN_CHUNK = 1408
SEL_TILE = 512
RWKV_CHUNK = 64
RWKV_PACK = 4
RWKV_STEP = 256
SGU_TILE = 512
VMEM_LIMIT = 56 * 1024 * 1024


def _cparams(*sem):
    return pltpu.CompilerParams(dimension_semantics=sem, vmem_limit_bytes=VMEM_LIMIT)


def _dot(a, b):
    return jnp.dot(a.astype(BF16), b.astype(BF16), preferred_element_type=F32)


def _dot_nt(a, b):
    return lax.dot_general(a.astype(BF16), b.astype(BF16), (((1,), (1,)), ((), ())),
                           preferred_element_type=F32)


def _dot_tn(a, b):
    return lax.dot_general(a.astype(BF16), b.astype(BF16), (((0,), (0,)), ((), ())),
                           preferred_element_type=F32)


def _split(x):
    hi = x.astype(BF16)
    lo = (x - hi.astype(F32)).astype(BF16)
    return hi, lo


def _dot_x2(a, b):
    hi, lo = _split(a)
    return _dot(hi, b) + _dot(lo, b)


def _sigmoid(x):
    return 1.0 / (1.0 + jnp.exp(-x))


def _silu(x):
    return x * _sigmoid(x)


def _rms(x, g):
    return x * lax.rsqrt(jnp.mean(x * x, axis=-1, keepdims=True) + RMS_EPS) * g


def _pre(x, g, mod_ref):
    return _rms(x, g) * (1.0 + mod_ref[1:2, :]) + mod_ref[0:1, :]


def _ada_kernel(c_ref, w_ref, b_ref, o_ref):
    cond = _silu(c_ref[...])
    o_ref[...] = _dot(cond, w_ref[...]) + b_ref[...]


def _ada_mod(c, ada_w, ada_b):
    L, D, N = ada_w.shape
    B = c.shape[0]
    tn = 1536
    return pl.pallas_call(
        _ada_kernel,
        grid=(L, N // tn),
        in_specs=[
            pl.BlockSpec((B, D), lambda l, j: (0, 0)),
            pl.BlockSpec((None, D, tn), lambda l, j: (l, 0, j)),
            pl.BlockSpec((None, 1, tn), lambda l, j: (l, 0, j)),
        ],
        out_specs=pl.BlockSpec((None, B, tn), lambda l, j: (l, 0, j)),
        out_shape=jax.ShapeDtypeStruct((L, B, N), F32),
        compiler_params=_cparams("arbitrary", "arbitrary"),
        name="ada_mod",
    )(c, ada_w, ada_b.reshape(L, 1, N))


def _ffn_kernel(x_ref, mod_ref, pg_ref, qg_ref, win_ref, wout_ref, o_ref):
    x = x_ref[...]
    h = _pre(x, pg_ref[...], mod_ref).astype(BF16)
    acc = None
    for c in range(D_FF // FFN_CHUNK):
        lo, hi = c * FFN_CHUNK, (c + 1) * FFN_CHUNK
        gate = jnp.dot(h, win_ref[:, lo:hi], preferred_element_type=F32)
        up = jnp.dot(h, win_ref[:, D_FF + lo:D_FF + hi], preferred_element_type=F32)
        act = (_silu(gate) * up).astype(BF16)
        part = jnp.dot(act, wout_ref[lo:hi, :], preferred_element_type=F32)
        acc = part if acc is None else acc + part
    o_ref[...] = x + MACARON_WEIGHT * mod_ref[2:3, :] * _rms(acc, qg_ref[...])


def _const_spec(shape):
    nd = len(shape)
    return pl.BlockSpec(shape, lambda *_: (0,) * nd, pipeline_mode=pl.Buffered(1))


def _ffn(x, mod3, pre_g, post_g, w_in, w_out):
    B, S, D = x.shape
    tm = min(FFN_TILE, S)
    return pl.pallas_call(
        _ffn_kernel,
        grid=(B, S // tm),
        in_specs=[
            pl.BlockSpec((None, tm, D), lambda b, i: (b, i, 0)),
            pl.BlockSpec((None, 3, D), lambda b, i: (b, 0, 0)),
            _const_spec((1, D)),
            _const_spec((1, D)),
            _const_spec((D, 2 * D_FF)),
            _const_spec((D_FF, D)),
        ],
        out_specs=pl.BlockSpec((None, tm, D), lambda b, i: (b, i, 0)),
        out_shape=jax.ShapeDtypeStruct((B, S, D), F32),
        compiler_params=_cparams("parallel", "parallel"),
        name="ffn",
    )(x, mod3, pre_g.reshape(1, D), post_g.reshape(1, D), w_in, w_out)


Q_AUG_COLS = NSA_HEADS * NSA_AUG
V_AUG_COLS = NSA_GROUPS * NSA_VAUG
CMP_COLS = 2 * NSA_GROUPS * NSA_DK
GT_AUG_COLS = NSA_GROUPS * LANES
PROJ_WIDTHS = (Q_AUG_COLS, CMP_COLS, V_AUG_COLS, V_AUG_COLS, GT_AUG_COLS, 2 * SGU_WIDTH, RWKV_COLS)
PROJ_COLS = sum(PROJ_WIDTHS)
KT_ROWS = 2 * NSA_GROUPS * NSA_AUG


def _proj_kernel(x_ref, mod_ref, pg_ref, w_ref, wkt_ref, qc_ref, vc_ref,
                 q_ref, cmp_ref, vs_ref, vw_ref, gt_ref, sgu_ref, rw_ref, ks_ref, kw_ref):
    h = _pre(x_ref[...], pg_ref[...], mod_ref).astype(BF16)
    tm = h.shape[0]
    edges = np.cumsum((0,) + PROJ_WIDTHS)
    cols = [jnp.dot(h, w_ref[:, int(a):int(b)], preferred_element_type=F32)
            for a, b in zip(edges[:-1], edges[1:])]
    q_ref[...] = (cols[0] * NSA_DK ** -0.5 + qc_ref[...]).astype(BF16)
    cmp_ref[...] = cols[1].astype(BF16)
    vs_ref[...] = (cols[2] + vc_ref[...]).astype(BF16)
    vw_ref[...] = (cols[3] + vc_ref[...]).astype(BF16)
    gt_ref[...] = cols[4]
    sgu_ref[...] = cols[5]
    rw_ref[...] = cols[6]
    kt = _dot_nt(wkt_ref[...], h)
    pos = pl.program_id(1) * tm + lax.broadcasted_iota(jnp.int32, (1, tm), 1)
    row = lax.broadcasted_iota(jnp.int32, (NSA_AUG, 1), 0)
    pos_rows = jnp.where(row == NSA_DK, (pos // POS_SPLIT).astype(F32),
                         jnp.where(row == NSA_DK + 1, (pos % POS_SPLIT).astype(F32), 0.0))
    for g in range(NSA_GROUPS):
        ks_ref[g, 0] = (kt[g * NSA_AUG:(g + 1) * NSA_AUG] + pos_rows).astype(BF16)
        kw = (kt[(NSA_GROUPS + g) * NSA_AUG:(NSA_GROUPS + g + 1) * NSA_AUG] + pos_rows).astype(BF16)
        for t in range(tm // LANES):
            kw_ref[g, t] = kw[:, t * LANES:(t + 1) * LANES]


def _proj_weights(w):
    D = w.shape[0]
    G, DK = NSA_GROUPS, NSA_DK
    q, kc, vc, ks, vs, kw, vw = [w[:, a:b] for a, b in zip(
        np.cumsum((0, Q_COLS) + (G * DK,) * 5), np.cumsum((Q_COLS,) + (G * DK,) * 6))]
    gt = w[:, Q_COLS + KV_COLS:NSA_COLS]

    def padded(z, width):
        n = z.shape[1] // DK
        return jnp.pad(z.reshape(D, n, DK), ((0, 0), (0, 0), (0, width - DK))).reshape(D, n * width)

    def padded_groups(z):
        n = z.shape[1] // G
        return jnp.pad(z.reshape(D, G, n), ((0, 0), (0, 0), (0, LANES - n))).reshape(D, G * LANES)

    token_major = jnp.concatenate(
        [padded(q, NSA_AUG), kc, vc, padded(vs, NSA_VAUG), padded(vw, NSA_VAUG),
         padded_groups(gt), w[:, NSA_COLS:NSA_COLS + 2 * SGU_WIDTH + RWKV_COLS]],
        axis=1).astype(BF16)
    keys_t = jnp.concatenate([padded(ks, NSA_AUG), padded(kw, NSA_AUG)], axis=1).T.astype(BF16)
    return token_major, keys_t


def _mix_proj(x, mod3, pre_g, w, wkt):
    B, S, D = x.shape
    tm = min(FFN_TILE, S)
    slopes = 2.0 ** -np.arange(1, NSA_HEADS + 1, dtype=np.float32)
    q_const = np.zeros((NSA_HEADS, NSA_AUG), np.float32)
    q_const[:, NSA_DK] = POS_SPLIT * slopes
    q_const[:, NSA_DK + 1] = slopes
    v_const = np.zeros((NSA_GROUPS, NSA_VAUG), np.float32)
    v_const[:, NSA_DK] = 1.0
    dtypes = (BF16, BF16, BF16, BF16, F32, F32, F32)
    tok = lambda n: pl.BlockSpec((None, tm, n), lambda b, i: (b, i, 0))
    return pl.pallas_call(
        _proj_kernel,
        grid=(B, S // tm),
        in_specs=[
            tok(D),
            pl.BlockSpec((None, 3, D), lambda b, i: (b, 0, 0)),
            _const_spec((1, D)),
            _const_spec((D, PROJ_COLS)),
            _const_spec((KT_ROWS, D)),
            _const_spec((1, Q_AUG_COLS)),
            _const_spec((1, V_AUG_COLS)),
        ],
        out_specs=[tok(n) for n in PROJ_WIDTHS] + [
            pl.BlockSpec((None, NSA_GROUPS, 1, NSA_AUG, tm), lambda b, i: (b, 0, i, 0, 0)),
            pl.BlockSpec((None, NSA_GROUPS, tm // LANES, NSA_AUG, LANES), lambda b, i: (b, 0, i, 0, 0))],
        out_shape=[jax.ShapeDtypeStruct((B, S, n), dt) for n, dt in zip(PROJ_WIDTHS, dtypes)] + [
            jax.ShapeDtypeStruct((B, NSA_GROUPS, S // tm, NSA_AUG, tm), BF16),
            jax.ShapeDtypeStruct((B, NSA_GROUPS, S // LANES, NSA_AUG, LANES), BF16)],
        compiler_params=_cparams("parallel", "parallel"),
        name="mix_proj",
    )(x, mod3, pre_g.reshape(1, D), w, wkt,
      jnp.asarray(q_const.reshape(1, -1)), jnp.asarray(v_const.reshape(1, -1)))


def _cmp_kernel(xk_ref, xv_ref, pek_ref, pev_ref, w1k_ref, w1v_ref, w2kt_ref, w2v_ref,
                kct_ref, vc_ref):
    half = (CMP_LEN // 2) * NSA_DK

    def hidden(x_ref, pe_ref, w1_ref):
        x = x_ref[...].astype(F32)
        n = x.shape[0]
        top = _dot(x + pe_ref[0:1, :], w1_ref[0:half, :])
        bot = _dot(x + pe_ref[1:2, :], w1_ref[half:2 * half, :])
        return _silu(top + pltpu.roll(bot, n - 1, axis=0))

    kct_ref[...] = _dot_nt(w2kt_ref[...], hidden(xk_ref, pek_ref, w1k_ref)).astype(BF16)
    vc_ref[...] = _dot(hidden(xv_ref, pev_ref, w1v_ref), w2v_ref[...]).astype(BF16)


def _compress(xk, xv, pe_k, pe_v, w1k, w1v, w2kt, w2v):
    B, G, NC, F = xk.shape
    blk = lambda *s: pl.BlockSpec((None, None) + s, lambda b, g: (b, g, 0, 0))
    return pl.pallas_call(
        _cmp_kernel,
        grid=(B, G),
        in_specs=[blk(NC, F), blk(NC, F), _const_spec((2, F)), _const_spec((2, F)),
                  _const_spec((2 * F, CMP_HID)), _const_spec((2 * F, CMP_HID)),
                  _const_spec((NSA_DK, CMP_HID)), _const_spec((CMP_HID, NSA_VAUG))],
        out_specs=[blk(NSA_DK, NC), blk(NC, NSA_VAUG)],
        out_shape=[jax.ShapeDtypeStruct((B, G, NSA_DK, NC), BF16),
                   jax.ShapeDtypeStruct((B, G, NC, NSA_VAUG), BF16)],
        compiler_params=_cparams("parallel", "parallel"),
        name="nsa_compress",
    )(xk, xv, pe_k, pe_v, w1k, w1v, w2kt, w2v)


def _masked_softmax(s, valid):
    s = jnp.where(valid, s, NEG_INF)
    m = jnp.max(s, axis=-1, keepdims=True)
    p = jnp.where(valid, jnp.exp(s - m), 0.0)
    l = jnp.sum(p, axis=-1, keepdims=True)
    return p * (1.0 / jnp.where(l > 0.0, l, 1.0))


def _nsa_kernel(q_ref, gt_ref, kc_ref, vc_ref, ks_ref, vs_ref, kw_ref, vw_ref, selt_ref, e_ref,
                o_ref, imp_ref, tiles_ref, *, n_top, tk):
    s0 = pl.program_id(2) * NSA_STEP
    R = NSA_HPG * Q_BLOCK
    n_slc = selt_ref.shape[0]
    n_cmp = kc_ref.shape[-1]
    parts = range(NSA_STEP // Q_BLOCK)
    each = lambda fn, *lists: [fn(*xs) for xs in zip(*lists)]
    rows = lambda c: slice(c * Q_BLOCK, (c + 1) * Q_BLOCK)
    head_cols = lambda h: slice(h * NSA_AUG, (h + 1) * NSA_AUG)
    q = [jnp.concatenate([q_ref[rows(c), head_cols(h)] for h in range(NSA_HPG)], axis=0) for c in parts]
    q_row = lax.broadcasted_iota(jnp.int32, (R, 1), 0) & (Q_BLOCK - 1)
    t_i = [s0 + c * Q_BLOCK + q_row for c in parts]

    def normalised(acc):
        return acc * (1.0 / acc[:, NSA_DK:NSA_DK + 1])

    cmp_end = lax.broadcasted_iota(jnp.int32, (1, n_cmp), 1) * CMP_STRIDE + (CMP_LEN - 1)
    s_c = [jnp.dot(x, kc_ref[...], preferred_element_type=F32) for x in q]
    p_c = each(lambda s, t: _masked_softmax(s, cmp_end <= t), s_c, t_i)
    o_c = [_dot(p, vc_ref[...]) for p in p_c]

    j = lax.broadcasted_iota(jnp.int32, (n_slc, Q_BLOCK), 0)
    imp = []
    for c in parts:
        p_sum = p_c[c][0:Q_BLOCK]
        for h in range(1, NSA_HPG):
            p_sum = p_sum + p_c[c][h * Q_BLOCK:(h + 1) * Q_BLOCK]
        hi, lo = _split(p_sum)
        raw = _dot_nt(selt_ref[...], hi) + _dot_nt(selt_ref[...], lo)
        tq = s0 + c * Q_BLOCK + lax.broadcasted_iota(jnp.int32, (n_slc, Q_BLOCK), 1)
        cur = tq // SLC_BLOCK
        forced = (j == 0) | (j == cur) | (j == cur - 1)
        live = j * SLC_BLOCK <= tq
        imp.append(jnp.where(forced, FORCE_SCORE, jnp.where(live, raw, NEG_INF)))
        imp_ref[c] = imp[c]

    def rank_body(i, cnt):
        tie = jnp.where(j > i, 1.0, 0.0)
        out = []
        for c in parts:
            r = imp_ref[c, pl.ds(i, 1), :]
            out.append(cnt[c] + jnp.where(r > imp[c], 1.0, jnp.where(r == imp[c], tie, 0.0)))
        return tuple(out)

    n_live = (s0 + NSA_STEP) // SLC_BLOCK
    cnt = lax.fori_loop(0, n_live, rank_body, tuple(jnp.zeros((n_slc, Q_BLOCK), F32) for _ in parts))
    lhs = []
    for c in parts:
        dropped = jnp.concatenate(
            [jnp.where(cnt[c] < n_top, 0.0, 1.0), jnp.zeros((NSA_AUG - n_slc, Q_BLOCK), F32)], axis=0)
        dropped = dropped.T.astype(BF16)
        lhs.append(jnp.concatenate([q[c], jnp.concatenate([dropped] * NSA_HPG, axis=0)], axis=1))

    last = (s0 + NSA_STEP - 1) // tk
    chosen = jnp.where(cnt[0] < n_top, 1.0, 0.0)
    for c in parts[1:]:
        chosen = jnp.maximum(chosen, jnp.where(cnt[c] < n_top, 1.0, 0.0))
    blocks_per_tile = tk // SLC_BLOCK
    n_used = jnp.int32(0)
    for kt in range(n_slc // blocks_per_tile):
        used = jnp.max(chosen[kt * blocks_per_tile:(kt + 1) * blocks_per_tile, :]) > 0.0
        tiles_ref[n_used] = kt
        n_used = n_used + jnp.where(used & (kt < last), 1, 0).astype(jnp.int32)

    def sel_tile(kt, carry, diagonal):
        m, acc = carry[0::2], carry[1::2]
        rhs = jnp.concatenate([ks_ref[kt], e_ref[kt]], axis=0)
        v = vs_ref[pl.ds(pl.multiple_of(kt * tk, tk), tk), :]
        s = [jnp.dot(x, rhs, preferred_element_type=F32) for x in lhs]
        if diagonal:
            pos = kt * tk + lax.broadcasted_iota(jnp.int32, (1, tk), 1)
            s = each(lambda x, t: jnp.where(pos <= t, x, NEG_INF), s, t_i)
        m_new = each(lambda x, y: jnp.maximum(x, jnp.max(y, axis=-1, keepdims=True)), m, s)
        alpha = each(lambda x, y: jnp.exp(x - y), m, m_new)
        p = each(lambda x, y: jnp.exp(x - y).astype(BF16), s, m_new)
        acc = each(lambda a, x, y: a * x + jnp.dot(y, v, preferred_element_type=F32), alpha, acc, p)
        out = []
        for x, y in zip(m_new, acc):
            out += [x, y]
        return tuple(out)

    init = []
    for _ in parts:
        init += [jnp.full((R, 1), NEG_INF, F32), jnp.zeros((R, NSA_VAUG), F32)]
    carry = lax.fori_loop(0, n_used, lambda i, x: sel_tile(tiles_ref[i], x, False), tuple(init))
    o_s = [normalised(x) for x in sel_tile(last, carry, True)[1::2]]

    n_wt = (WINDOW + Q_BLOCK) // LANES
    lane_pos = lax.broadcasted_iota(jnp.int32, (1, LANES), 1)
    s_w, start = [], []
    for c in parts:
        start.append(jnp.maximum(s0 + c * Q_BLOCK - WINDOW, 0))
        tiles = []
        for i in range(n_wt):
            pos = start[c] + i * LANES + lane_pos
            x = jnp.dot(q[c], kw_ref[start[c] // LANES + i], preferred_element_type=F32)
            x = jnp.where(pos <= t_i[c], x, NEG_INF)
            if i == 0:
                x = jnp.where(pos > t_i[c] - WINDOW, x, NEG_INF)
            tiles.append(x)
        s_w.append(jnp.concatenate(tiles, axis=1))
    p_w = [jnp.exp(x - jnp.max(x, axis=-1, keepdims=True)) for x in s_w]
    o_w = [normalised(_dot(p, vw_ref[pl.ds(pl.multiple_of(st, LANES), n_wt * LANES), :]))
           for p, st in zip(p_w, start)]

    for c in parts:
        gate = _sigmoid(gt_ref[rows(c), :])
        for h in range(NSA_HPG):
            hr = slice(h * Q_BLOCK, (h + 1) * Q_BLOCK)
            o = (gate[:, 3 * h:3 * h + 1] * o_c[c][hr] + gate[:, 3 * h + 1:3 * h + 2] * o_s[c][hr]
                 + gate[:, 3 * h + 2:3 * h + 3] * o_w[c][hr])
            o_ref[rows(c), head_cols(h)] = o.astype(o_ref.dtype)


def _selection_map_t(n_cmp_rows, n_slc):
    rs, rc = SLC_BLOCK // CMP_STRIDE, CMP_LEN // CMP_STRIDE
    m = np.zeros((n_slc, n_cmp_rows), np.float32)
    for jj in range(n_slc):
        for a in range(rs):
            for b in range(rc):
                i = rs * jj - a - b
                if 0 <= i < n_cmp_rows - 1:
                    m[jj, i] += 1.0
    return jnp.asarray(m, BF16)


def _nsa_attention(q, gt, kc, vc, ks, vs, kw, vw):
    B, S, _ = q.shape
    G, A = NSA_GROUPS, NSA_AUG
    n_slc = S // SLC_BLOCK
    n_cmp = kc.shape[-1]
    n_kt, _, tk = ks.shape[2:]
    selt = _selection_map_t(n_cmp, n_slc)
    key_blk = np.arange(S).reshape(n_kt, 1, tk) // SLC_BLOCK
    e = np.where(np.arange(A).reshape(1, A, 1) == key_blk, -MASK_BIG, 0.0)
    heads = pl.BlockSpec((None, NSA_STEP, NSA_HPG * A), lambda b, g, i: (b, i, g))
    values = pl.BlockSpec((None, S, NSA_VAUG), lambda b, g, i: (b, 0, g))
    per_g = lambda *s: pl.BlockSpec((None, None) + s, lambda b, g, i: (b, g) + (0,) * len(s))
    kern = functools.partial(_nsa_kernel, n_top=min(SLC_TOPN, n_slc), tk=tk)
    return pl.pallas_call(
        kern,
        grid=(B, G, S // NSA_STEP),
        in_specs=[heads, pl.BlockSpec((None, NSA_STEP, LANES), lambda b, g, i: (b, i, g)),
                  per_g(A, n_cmp), per_g(n_cmp, NSA_VAUG),
                  per_g(n_kt, A, tk), values, per_g(S // LANES, A, LANES), values,
                  pl.BlockSpec((n_slc, n_cmp), lambda b, g, i: (0, 0)),
                  pl.BlockSpec((n_kt, A, tk), lambda b, g, i: (0, 0, 0))],
        out_specs=heads,
        out_shape=jax.ShapeDtypeStruct((B, S, NSA_HEADS * NSA_VAUG), BF16),
        scratch_shapes=[pltpu.VMEM((NSA_STEP // Q_BLOCK, n_slc, Q_BLOCK), F32),
                        pltpu.SMEM((n_kt,), jnp.int32)],
        compiler_params=_cparams("parallel", "parallel", "arbitrary"),
        name="nsa_attention",
    )(q, gt, kc, vc, ks, vs, kw, vw, selt, jnp.asarray(e, BF16))


def _with_position_rows(kt, pos):
    lead = kt.shape[:-2]
    n = kt.shape[-1]
    hi = jnp.broadcast_to(jnp.asarray(pos // POS_SPLIT, BF16)[..., None, :], lead + (1, n))
    lo = jnp.broadcast_to(jnp.asarray(pos % POS_SPLIT, BF16)[..., None, :], lead + (1, n))
    pad = jnp.zeros(lead + (NSA_AUG - NSA_DK - 2, n), BF16)
    return jnp.concatenate([kt, hi, lo, pad], axis=-2)


def _nsa_mixer(q, kvc, vs, vw, gt, ks, kw, k_w1, k_w2, k_pe, v_w1, v_w2, v_pe):
    B, S, _ = q.shape
    G, DK = NSA_GROUPS, NSA_DK
    half = CMP_LEN // 2
    nc = S // half

    def half_blocks(z):
        z = z.reshape(B, nc, half, G, DK)
        return z.transpose(0, 3, 1, 2, 4).reshape(B, G, nc, half * DK)

    kct, vcc = _compress(
        half_blocks(kvc[:, :, :G * DK]), half_blocks(kvc[:, :, G * DK:]),
        k_pe.reshape(2, half * DK), v_pe.reshape(2, half * DK),
        k_w1.astype(BF16), v_w1.astype(BF16), k_w2.T.astype(BF16),
        jnp.pad(v_w2, ((0, 0), (0, NSA_VAUG - DK))).astype(BF16))
    kca = _with_position_rows(kct, np.arange(nc) * CMP_STRIDE + (CMP_LEN - 1))
    return _nsa_attention(q, gt, kca, vcc, ks, vs, kw, vw)


def _gelu_tanh(x):
    return 0.5 * x * (1.0 + jnp.tanh(np.sqrt(2.0 / np.pi).astype(np.float32) * (x + 0.044715 * (x * x * x))))


def _sgu_kernel(p_ref, g_ref, b_ref, w_ref, bs_ref, o_ref):
    ts = p_ref.shape[0]
    gd = SGU_WIDTH // SGU_GROUPS
    u = _gelu_tanh(p_ref[:, 0:SGU_WIDTH])
    v = _gelu_tanh(p_ref[:, SGU_WIDTH:2 * SGU_WIDTH])
    mu = jnp.mean(v, axis=-1, keepdims=True)
    vc = v - mu
    var = jnp.mean(vc * vc, axis=-1, keepdims=True)
    vn = (vc * lax.rsqrt(var + LN_EPS) * g_ref[...] + b_ref[...]).astype(BF16)
    causal = (lax.broadcasted_iota(jnp.int32, (SGU_CHUNK, SGU_CHUNK), 0)
              >= lax.broadcasted_iota(jnp.int32, (SGU_CHUNK, SGU_CHUNK), 1))
    ws = [jnp.where(causal, w_ref[i], 0.0).astype(BF16) for i in range(SGU_GROUPS)]
    for n in range(ts // SGU_CHUNK):
        r0, r1 = n * SGU_CHUNK, (n + 1) * SGU_CHUNK
        s = jnp.concatenate(
            [jnp.dot(ws[i], vn[r0:r1, i * gd:(i + 1) * gd], preferred_element_type=F32)
             for i in range(SGU_GROUPS)], axis=1) + bs_ref[...]
        o_ref[r0:r1, :] = (u[r0:r1] * s).astype(o_ref.dtype)


def _sgu_mixer(p, ln_g, ln_b, w_s, b_s):
    B, S, _ = p.shape
    ts = min(SGU_TILE, S)
    gd = SGU_WIDTH // SGU_GROUPS
    bias = jnp.repeat(b_s.T, gd, axis=1)
    return pl.pallas_call(
        _sgu_kernel,
        grid=(B, S // ts),
        in_specs=[pl.BlockSpec((None, ts, 2 * SGU_WIDTH), lambda b, i: (b, i, 0)),
                  _const_spec((1, SGU_WIDTH)), _const_spec((1, SGU_WIDTH)),
                  _const_spec((SGU_GROUPS, SGU_CHUNK, SGU_CHUNK)), _const_spec((SGU_CHUNK, SGU_WIDTH))],
        out_specs=pl.BlockSpec((None, ts, SGU_WIDTH), lambda b, i: (b, i, 0)),
        out_shape=jax.ShapeDtypeStruct((B, S, SGU_WIDTH), BF16),
        compiler_params=_cparams("parallel", "parallel"),
        name="sgu",
    )(p, ln_g.reshape(1, -1), ln_b.reshape(1, -1), w_s, bias)


def _head_sum(x, bd_ref):
    return _dot_x2(x, bd_ref[...])


def _rwkv_operands(p_ref, carry, mu_ref, w0_ref, w2_ref, a0_ref, a2_ref, g2_ref, kk_ref, ka_ref, bd_ref,
                   v_res_refs):
    W = RWKV_WIDTH
    p = p_ref[...]
    tr = p.shape[0]

    @pl.when(pl.program_id(1) == 0)
    def _():
        carry[...] = jnp.zeros_like(carry)

    rowi = lax.broadcasted_iota(jnp.int32, (tr, 1), 0)
    p_prev = jnp.where(rowi == 0, carry[0:1, :], pltpu.roll(p, 1, axis=0))
    carry[0:1, :] = p[tr - 1:tr, :]
    ps = p + (p_prev - p) * mu_ref[...]
    r, k, v = ps[:, 0:W], ps[:, W:2 * W], ps[:, 2 * W:3 * W]
    wa = ps[:, 3 * W:3 * W + DECAY_LORA + AAA_LORA]
    gd = ps[:, 3 * W + DECAY_LORA + AAA_LORA:]
    w = w0_ref[...] + _dot(jnp.tanh(wa), w2_ref[...])
    x = -w
    softplus = jnp.maximum(x, 0.0) + jnp.log(1.0 + jnp.exp(-jnp.abs(x)))
    ld = -jnp.exp(-softplus - 0.5)
    a = _sigmoid(a0_ref[...] + _dot(wa, a2_ref[...]))
    g = _dot(_sigmoid(gd), g2_ref[...])
    if v_res_refs is not None:
        vf_ref, v0_ref, v1_ref, v2_ref = v_res_refs
        lora = _dot(_dot(v, v1_ref[...]), v2_ref[...])
        v = v + (vf_ref[...] - v) * _sigmoid(v0_ref[...] + lora)
    kk = k * kk_ref[...]
    norm = jnp.sqrt(_head_sum(kk * kk, bd_ref))
    kn = kk * (1.0 / jnp.maximum(norm, 1e-12))
    return r, ld, k * (1.0 + (a - 1.0) * ka_ref[...]), v, kn, kn * a, g


def _head_block_diag():
    idx = np.arange(RWKV_WIDTH) // RWKV_HEAD_DIM
    return jnp.asarray(idx[:, None] == idx[None, :], BF16)


def _rwkv_kernel(*refs, has_vres):
    (p_ref, mu_ref, w0_ref, w2_ref, a0_ref, a2_ref, g2_ref, kk_ref, ka_ref,
     rk_ref, lg_ref, lb_ref, bd_ref) = refs[:13]
    if has_vres:
        v_res_refs = refs[13:17]
        o_ref, state, carry = refs[17:]
    else:
        v_res_refs = None
        o_ref, v_ref, state, carry = refs[13:]
    r, ld, k, v, kn, ba, g = _rwkv_operands(p_ref, carry, mu_ref, w0_ref, w2_ref, a0_ref, a2_ref, g2_ref,
                                            kk_ref, ka_ref, bd_ref, v_res_refs)
    if not has_vres:
        v_ref[...] = v
    L = RWKV_CHUNK
    N = RWKV_HEAD_DIM
    NH = RWKV_PACK
    QW = NH * N
    n_sub = r.shape[0] // L

    @pl.when(pl.program_id(1) == 0)
    def _():
        state[...] = jnp.zeros_like(state)

    tt = lax.broadcasted_iota(jnp.int32, (n_sub * L, n_sub * L), 0)
    ts = lax.broadcasted_iota(jnp.int32, (n_sub * L, n_sub * L), 1)
    tril_ones = jnp.where((tt >= ts) & (tt // L == ts // L), 1.0, 0.0).astype(BF16)
    own = (lax.broadcasted_iota(jnp.int32, (QW, QW), 0) // N
           == lax.broadcasted_iota(jnp.int32, (QW, QW), 1) // N)
    ti = lax.broadcasted_iota(jnp.int32, (L, QW), 0)
    si = lax.broadcasted_iota(jnp.int32, (L, QW), 1) % L
    strict = ti > si
    incl = ti >= si
    eye_tok = jnp.where(ti == si, 1.0, 0.0)
    eye_ch = jnp.where(lax.broadcasted_iota(jnp.int32, (QW, QW), 0)
                       == lax.broadcasted_iota(jnp.int32, (QW, QW), 1), 1.0, 0.0)

    def stacked(x):
        x = x.astype(BF16)
        return jnp.where(own, jnp.concatenate([x] * NH, axis=0), jnp.zeros((), BF16))

    h1 = ld.astype(BF16)
    r1 = ld - h1.astype(F32)
    h2 = r1.astype(BF16)
    h3 = (r1 - h2.astype(F32)).astype(BF16)
    cum = _dot(tril_ones, h1) + (_dot(tril_ones, h2) + _dot(tril_ones, h3))
    w_in = jnp.exp(cum)
    w_inv = jnp.exp(-cum)
    rt = r * w_in
    kt = k * w_inv
    at = -kn * jnp.exp(cum - ld)
    bt = ba * w_inv

    n_grp = RWKV_WIDTH // QW
    chains = [(c, q) for c in range(n_sub) for q in range(n_grp)]
    each = lambda fn, *lists: [fn(*xs) for xs in zip(*lists)]
    pieces = lambda z: [z[c * L:(c + 1) * L, q * QW:(q + 1) * QW] for c, q in chains]
    cat0 = lambda *xs: jnp.concatenate(xs, axis=0)
    cat1 = lambda *xs: jnp.concatenate(xs, axis=1)
    a_t, r_t, b_t, k_t, v_t = pieces(at), pieces(rt), pieces(bt), pieces(kt), pieces(v)
    v_st = [stacked(x) for x in v_t]
    gram = each(lambda a, r, b, k: _dot_nt(cat0(a, r), cat0(stacked(b), stacked(k))), a_t, r_t, b_t, k_t)
    a_ab = [jnp.where(strict, x[0:L, 0:QW], 0.0) for x in gram]
    a_ak = [jnp.where(strict, x[0:L, QW:2 * QW], 0.0) for x in gram]
    m_rb = [jnp.where(incl, x[L:2 * L, 0:QW], 0.0) for x in gram]
    m_rk = [jnp.where(incl, x[L:2 * L, QW:2 * QW], 0.0) for x in gram]
    t_inv = [eye_tok + x for x in a_ab]
    power = each(lambda p: _dot(p, stacked(p)), a_ab)
    for _ in range(int(np.log2(L)) - 2):
        both = each(lambda p, t: _dot(cat0(p, t), stacked(p)), power, t_inv)
        power = [x[0:L] for x in both]
        t_inv = each(lambda t, x: t + x[L:2 * L], t_inv, both)
    t_inv = each(lambda t, p: t + _dot(t, stacked(p)), t_inv, power)
    akv = each(_dot, a_ak, v_st)
    au = each(lambda t, a, x: _dot(t, cat1(stacked(a), stacked(x))), t_inv, a_t, akv)
    ahat = [x[:, 0:QW] for x in au]
    uhat = [x[:, QW:2 * QW] for x in au]
    rhat = each(lambda r, m, a: r + _dot(m, stacked(a)), r_t, m_rb, ahat)
    y0 = each(lambda mb, mk, u, x: _dot(cat1(mb, mk), cat0(stacked(u), x)), m_rb, m_rk, uhat, v_st)
    w_last = [w_in[(c + 1) * L - 1:(c + 1) * L, q * QW:(q + 1) * QW] for c, q in chains]
    p_mat = each(lambda a, b, w: jnp.where(own, eye_ch + _dot_tn(a, b), 0.0) * w, ahat, b_t, w_last)
    q_mat = each(lambda u, x, b, k, w: jnp.where(own, _dot_tn(cat0(u, x), cat0(b, k)), 0.0) * w,
                 uhat, v_t, b_t, k_t, w_last)

    y_chunks = []
    for c in range(n_sub):
        idx = range(c * n_grp, (c + 1) * n_grp)
        s_old = [state[q] for q in range(n_grp)]
        y_chunks.append(jnp.concatenate([_dot_nt(rhat[i], s) + y0[i] for i, s in zip(idx, s_old)], axis=1))
        for q, i in enumerate(idx):
            state[q] = _dot(s_old[q], p_mat[i]) + q_mat[i]
    y = jnp.concatenate(y_chunks, axis=0)
    inv_n = 1.0 / N
    mean = _head_sum(y, bd_ref) * inv_n
    yc = y - mean
    var = _head_sum(yc * yc, bd_ref) * inv_n
    yn = yc * lax.rsqrt(var + LNX_EPS) * lg_ref[...] + lb_ref[...]
    bonus = _head_sum(r * k * rk_ref[...], bd_ref) * v
    o_ref[...] = ((yn + bonus) * g).astype(o_ref.dtype)


def _rwkv_mixer(p, mu, w0, w2, a0, a2, g2, k_k, k_a, r_k, lnx_g, lnx_b, v_first, v_res):
    B, S, C = p.shape
    W = RWKV_WIDTH
    step = min(RWKV_STEP, S)
    qw = RWKV_PACK * RWKV_HEAD_DIM
    row = lambda z: z.reshape(1, -1)
    lora_rows = DECAY_LORA + AAA_LORA
    w2p = jnp.zeros((lora_rows, W), F32).at[:DECAY_LORA].set(w2).astype(BF16)
    a2p = jnp.zeros((lora_rows, W), F32).at[DECAY_LORA:].set(a2).astype(BF16)
    tok = lambda n: pl.BlockSpec((None, step, n), lambda b, i: (b, i, 0))
    vec = _const_spec((1, W))
    args = [p, row(mu), row(w0), w2p, row(a0), a2p, g2.astype(BF16), row(k_k), row(k_a),
            row(r_k), row(lnx_g), row(lnx_b), _head_block_diag()]
    specs = [tok(C), _const_spec((1, C)), vec, _const_spec((lora_rows, W)), vec,
             _const_spec((lora_rows, W)), _const_spec((GATE_LORA, W)), vec, vec,
             vec, vec, vec, _const_spec((W, W))]
    out_specs = [tok(W)]
    out_shape = [jax.ShapeDtypeStruct((B, S, W), BF16)]
    if v_res is None:
        out_specs.append(tok(W))
        out_shape.append(jax.ShapeDtypeStruct((B, S, W), F32))
    else:
        v0, v1, v2 = v_res
        args += [v_first, row(v0), v1.astype(BF16), v2.astype(BF16)]
        specs += [tok(W), vec, _const_spec((W, MV_LORA)), _const_spec((MV_LORA, W))]
    out = pl.pallas_call(
        functools.partial(_rwkv_kernel, has_vres=v_res is not None),
        grid=(B, S // step),
        in_specs=specs,
        out_specs=out_specs,
        out_shape=out_shape,
        scratch_shapes=[pltpu.VMEM((W // qw, qw, qw), F32), pltpu.VMEM((8, C), F32)],
        compiler_params=_cparams("parallel", "arbitrary"),
        name="rwkv",
    )(*args)
    return out[0], (out[1] if v_res is None else v_first)


def _merge_kernel(x_ref, mod_ref, pg_ref, qg_ref, ya_ref, yb_ref, yc_ref, wg_ref,
                  wa_ref, wb_ref, wc_ref, ow_ref, o_ref):
    x = x_ref[...]
    D = x.shape[1]
    h = _pre(x, pg_ref[...], mod_ref).astype(BF16)
    merged = None
    for i, (y_ref, w_ref) in enumerate(((ya_ref, wa_ref), (yb_ref, wb_ref), (yc_ref, wc_ref))):
        gate = _sigmoid(jnp.dot(h, wg_ref[:, i * D:(i + 1) * D], preferred_element_type=F32))
        term = gate * jnp.dot(y_ref[...], w_ref[...], preferred_element_type=F32)
        merged = term if merged is None else merged + term
    y = _dot(merged, ow_ref[...])
    o_ref[...] = x + mod_ref[2:3, :] * _rms(y, qg_ref[...])


def _merge(x, mod3, pre_g, post_g, y_a, y_b, y_c, w_gate, branch_w, out_w):
    B, S, D = x.shape
    tm = min(FFN_TILE, S)
    tok = lambda n: pl.BlockSpec((None, tm, n), lambda b, i: (b, i, 0))
    w_a = jnp.pad(branch_w[0].reshape(NSA_HEADS, NSA_DK, D), ((0, 0), (0, NSA_VAUG - NSA_DK), (0, 0)))
    w_a = w_a.reshape(NSA_HEADS * NSA_VAUG, D)
    widths = (y_a.shape[-1], y_b.shape[-1], y_c.shape[-1])
    return pl.pallas_call(
        _merge_kernel,
        grid=(B, S // tm),
        in_specs=[tok(D), pl.BlockSpec((None, 3, D), lambda b, i: (b, 0, 0)),
                  _const_spec((1, D)), _const_spec((1, D))]
                 + [tok(n) for n in widths]
                 + [_const_spec((D, N_BRANCH * D))] + [_const_spec((n, D)) for n in widths]
                 + [_const_spec((D, D))],
        out_specs=tok(D),
        out_shape=jax.ShapeDtypeStruct((B, S, D), F32),
        compiler_params=_cparams("parallel", "parallel"),
        name="merge",
    )(x, mod3, pre_g.reshape(1, D), post_g.reshape(1, D), y_a, y_b, y_c, w_gate,
      w_a, branch_w[1], branch_w[2], out_w)


def kernel(x, c, ada_w, ada_b, pre_g, post_g, ffn_w_in, ffn_w_out, mix_w_in, branch_w, out_w,
           cmp_k_w1, cmp_k_w2, cmp_k_pe, cmp_v_w1, cmp_v_w2, cmp_v_pe,
           sgu_ln_g, sgu_ln_b, sgu_w, sgu_b,
           rwkv_mu, rwkv_w0, rwkv_w2, rwkv_a0, rwkv_a2, rwkv_g2, rwkv_kk, rwkv_ka, rwkv_rk,
           rwkv_lnx_g, rwkv_lnx_b, rwkv_v0, rwkv_v1, rwkv_v2):
    B, S, D = x.shape
    depth = ada_w.shape[0]
    mod = _ada_mod(c, ada_w, ada_b).reshape(depth, B, 3, 3, D)
    v_first = None
    for l in range(depth):
        x = _ffn(x, mod[l, :, 0], pre_g[l, 0], post_g[l, 0],
                 ffn_w_in[l, 0].astype(BF16), ffn_w_out[l, 0].astype(BF16))

        w = mix_w_in[l]
        n_in = NSA_COLS + 2 * SGU_WIDTH + RWKV_COLS
        q, kvc, vs, vw, gt, p_sgu, p_rwkv, ks, kw = _mix_proj(
            x, mod[l, :, 1], pre_g[l, 1], *_proj_weights(w))
        y_a = _nsa_mixer(q, kvc, vs, vw, gt, ks, kw, cmp_k_w1[l], cmp_k_w2[l], cmp_k_pe[l],
                         cmp_v_w1[l], cmp_v_w2[l], cmp_v_pe[l])
        y_b = _sgu_mixer(p_sgu, sgu_ln_g[l], sgu_ln_b[l], sgu_w[l], sgu_b[l])
        v_res = None if l == 0 else (rwkv_v0[l - 1], rwkv_v1[l - 1], rwkv_v2[l - 1])
        y_c, v_first = _rwkv_mixer(p_rwkv, rwkv_mu[l], rwkv_w0[l], rwkv_w2[l], rwkv_a0[l], rwkv_a2[l],
                                   rwkv_g2[l], rwkv_kk[l], rwkv_ka[l], rwkv_rk[l],
                                   rwkv_lnx_g[l], rwkv_lnx_b[l], v_first, v_res)
        x = _merge(x, mod[l, :, 1], pre_g[l, 1], post_g[l, 1], y_a, y_b, y_c,
                   w[:, n_in:].astype(BF16), branch_w[l].astype(BF16), out_w[l].astype(BF16))

        x = _ffn(x, mod[l, :, 2], pre_g[l, 2], post_g[l, 2],
                 ffn_w_in[l, 1].astype(BF16), ffn_w_out[l, 1].astype(BF16))
    return x
```

```python
import functools

import numpy as np
import jax
import jax.numpy as jnp
from jax import lax
from jax.experimental import pallas as pl
from jax.experimental.pallas import tpu as pltpu

D_MODEL = 1024
DEPTH = 2
NSA_HEADS = 8
NSA_GROUPS = 2
NSA_HPG = NSA_HEADS // NSA_GROUPS
NSA_DK = 64
CMP_LEN = 32
CMP_STRIDE = 16
CMP_HID = 256
SLC_BLOCK = 64
SLC_TOPN = 16
WINDOW = 512
Q_BLOCK = 128
SGU_CHUNK = 128
SGU_GROUPS = 4
SGU_WIDTH = 512
RWKV_HEADS = 8
RWKV_HEAD_DIM = 64
RWKV_WIDTH = RWKV_HEADS * RWKV_HEAD_DIM
DECAY_LORA = 64
AAA_LORA = 64
MV_LORA = 32
GATE_LORA = 128
N_BRANCH = 3
BRANCH_WIDTH = 512
D_FF = 2816
MACARON_WEIGHT = 0.5
RMS_EPS = 1e-6
LN_EPS = 1e-5
LNX_EPS = 64e-5
NEG_INF = -1e30
FORCE_SCORE = 1e4

NSA_COLS = NSA_HEADS * NSA_DK + 6 * NSA_GROUPS * NSA_DK + 3 * NSA_HEADS
RWKV_COLS = 3 * RWKV_WIDTH + DECAY_LORA + AAA_LORA + GATE_LORA
Q_COLS = NSA_HEADS * NSA_DK
KV_COLS = 6 * NSA_GROUPS * NSA_DK
GT_COLS = 3 * NSA_HEADS
LANES = 128
NSA_STEP = 256
NSA_VAUG = 128
NSA_AUG = 128
POS_SPLIT = 64
MASK_BIG = 2.0 ** 100

BF16 = jnp.bfloat16
F32 = jnp.float32

FFN_TILE = 512
FFN_CHUNK = 1408
SEL_TILE = 512
RWKV_CHUNK = 64
RWKV_PACK = 4
RWKV_STEP = 256
SGU_TILE = 512
VMEM_LIMIT = 56 * 1024 * 1024


def _cparams(*sem):
    return pltpu.CompilerParams(dimension_semantics=sem, vmem_limit_bytes=VMEM_LIMIT)


def _dot(a, b):
    return jnp.dot(a.astype(BF16), b.astype(BF16), preferred_element_type=F32)


def _dot_nt(a, b):
    return lax.dot_general(a.astype(BF16), b.astype(BF16), (((1,), (1,)), ((), ())),
                           preferred_element_type=F32)


def _dot_tn(a, b):
    return lax.dot_general(a.astype(BF16), b.astype(BF16), (((0,), (0,)), ((), ())),
                           preferred_element_type=F32)


def _split(x):
    hi = x.astype(BF16)
    lo = (x - hi.astype(F32)).astype(BF16)
    return hi, lo


def _dot_x2(a, b):
    hi, lo = _split(a)
    return _dot(hi, b) + _dot(lo, b)


def _sigmoid(x):
    return 1.0 / (1.0 + jnp.exp(-x))


def _silu(x):
    return x * _sigmoid(x)


def _rms(x, g):
    return x * lax.rsqrt(jnp.mean(x * x, axis=-1, keepdims=True) + RMS_EPS) * g


def _pre(x, g, mod_ref):
    return _rms(x, g) * (1.0 + mod_ref[1:2, :]) + mod_ref[0:1, :]


def _ada_kernel(c_ref, w_ref, b_ref, o_ref):
    cond = _silu(c_ref[...])
    o_ref[...] = _dot(cond, w_ref[...]) + b_ref[...]


def _ada_mod(c, ada_w, ada_b):
    L, D, N = ada_w.shape
    B = c.shape[0]
    tn = 1536
    return pl.pallas_call(
        _ada_kernel,
        grid=(L, N // tn),
        in_specs=[
            pl.BlockSpec((B, D), lambda l, j: (0, 0)),
            pl.BlockSpec((None, D, tn), lambda l, j: (l, 0, j)),
            pl.BlockSpec((None, 1, tn), lambda l, j: (l, 0, j)),
        ],
        out_specs=pl.BlockSpec((None, B, tn), lambda l, j: (l, 0, j)),
        out_shape=jax.ShapeDtypeStruct((L, B, N), F32),
        compiler_params=_cparams("arbitrary", "arbitrary"),
        name="ada_mod",
    )(c, ada_w, ada_b.reshape(L, 1, N))


def _ffn_kernel(x_ref, mod_ref, pg_ref, qg_ref, win_ref, wout_ref, o_ref):
    x = x_ref[...]
    h = _pre(x, pg_ref[...], mod_ref).astype(BF16)
    acc = None
    for c in range(D_FF // FFN_CHUNK):
        lo, hi = c * FFN_CHUNK, (c + 1) * FFN_CHUNK
        gate = jnp.dot(h, win_ref[:, lo:hi], preferred_element_type=F32)
        up = jnp.dot(h, win_ref[:, D_FF + lo:D_FF + hi], preferred_element_type=F32)
        act = (_silu(gate) * up).astype(BF16)
        part = jnp.dot(act, wout_ref[lo:hi, :], preferred_element_type=F32)
        acc = part if acc is None else acc + part
    o_ref[...] = x + MACARON_WEIGHT * mod_ref[2:3, :] * _rms(acc, qg_ref[...])


def _const_spec(shape):
    nd = len(shape)
    return pl.BlockSpec(shape, lambda *_: (0,) * nd, pipeline_mode=pl.Buffered(1))


def _ffn(x, mod3, pre_g, post_g, w_in, w_out):
    B, S, D = x.shape
    tm = min(FFN_TILE, S)
    return pl.pallas_call(
        _ffn_kernel,
        grid=(B, S // tm),
        in_specs=[
            pl.BlockSpec((None, tm, D), lambda b, i: (b, i, 0)),
            pl.BlockSpec((None, 3, D), lambda b, i: (b, 0, 0)),
            _const_spec((1, D)),
            _const_spec((1, D)),
            _const_spec((D, 2 * D_FF)),
            _const_spec((D_FF, D)),
        ],
        out_specs=pl.BlockSpec((None, tm, D), lambda b, i: (b, i, 0)),
        out_shape=jax.ShapeDtypeStruct((B, S, D), F32),
        compiler_params=_cparams("parallel", "parallel"),
        name="ffn",
    )(x, mod3, pre_g.reshape(1, D), post_g.reshape(1, D), w_in, w_out)


Q_AUG_COLS = NSA_HEADS * NSA_AUG
V_AUG_COLS = NSA_GROUPS * NSA_VAUG
CMP_COLS = 2 * NSA_GROUPS * NSA_DK
GT_AUG_COLS = NSA_GROUPS * LANES
PROJ_WIDTHS = (Q_AUG_COLS, CMP_COLS, V_AUG_COLS, V_AUG_COLS, GT_AUG_COLS, 2 * SGU_WIDTH, RWKV_COLS)
PROJ_COLS = sum(PROJ_WIDTHS)
KT_ROWS = 2 * NSA_GROUPS * NSA_AUG


def _proj_kernel(x_ref, mod_ref, pg_ref, w_ref, wkt_ref, qc_ref, vc_ref,
                 q_ref, cmp_ref, vs_ref, vw_ref, gt_ref, sgu_ref, rw_ref, ks_ref, kw_ref):
    h = _pre(x_ref[...], pg_ref[...], mod_ref).astype(BF16)
    tm = h.shape[0]
    edges = np.cumsum((0,) + PROJ_WIDTHS)
    cols = [jnp.dot(h, w_ref[:, int(a):int(b)], preferred_element_type=F32)
            for a, b in zip(edges[:-1], edges[1:])]
    q_ref[...] = (cols[0] * NSA_DK ** -0.5 + qc_ref[...]).astype(BF16)
    cmp_ref[...] = cols[1].astype(BF16)
    vs_ref[...] = (cols[2] + vc_ref[...]).astype(BF16)
    vw_ref[...] = (cols[3] + vc_ref[...]).astype(BF16)
    gt_ref[...] = cols[4]
    sgu_ref[...] = cols[5]
    rw_ref[...] = cols[6]
    kt = _dot_nt(wkt_ref[...], h)
    pos = pl.program_id(1) * tm + lax.broadcasted_iota(jnp.int32, (1, tm), 1)
    row = lax.broadcasted_iota(jnp.int32, (NSA_AUG, 1), 0)
    pos_rows = jnp.where(row == NSA_DK, (pos // POS_SPLIT).astype(F32),
                         jnp.where(row == NSA_DK + 1, (pos % POS_SPLIT).astype(F32), 0.0))
    for g in range(NSA_GROUPS):
        ks_ref[g, 0] = (kt[g * NSA_AUG:(g + 1) * NSA_AUG] + pos_rows).astype(BF16)
        kw = (kt[(NSA_GROUPS + g) * NSA_AUG:(NSA_GROUPS + g + 1) * NSA_AUG] + pos_rows).astype(BF16)
        for t in range(tm // LANES):
            kw_ref[g, t] = kw[:, t * LANES:(t + 1) * LANES]


def _proj_weights(w):
    D = w.shape[0]
    G, DK = NSA_GROUPS, NSA_DK
    q, kc, vc, ks, vs, kw, vw = [w[:, a:b] for a, b in zip(
        np.cumsum((0, Q_COLS) + (G * DK,) * 5), np.cumsum((Q_COLS,) + (G * DK,) * 6))]
    gt = w[:, Q_COLS + KV_COLS:NSA_COLS]

    def padded(z, width):
        n = z.shape[1] // DK
        return jnp.pad(z.reshape(D, n, DK), ((0, 0), (0, 0), (0, width - DK))).reshape(D, n * width)

    def padded_groups(z):
        n = z.shape[1] // G
        return jnp.pad(z.reshape(D, G, n), ((0, 0), (0, 0), (0, LANES - n))).reshape(D, G * LANES)

    token_major = jnp.concatenate(
        [padded(q, NSA_AUG), kc, vc, padded(vs, NSA_VAUG), padded(vw, NSA_VAUG),
         padded_groups(gt), w[:, NSA_COLS:NSA_COLS + 2 * SGU_WIDTH + RWKV_COLS]],
        axis=1).astype(BF16)
    keys_t = jnp.concatenate([padded(ks, NSA_AUG), padded(kw, NSA_AUG)], axis=1).T.astype(BF16)
    return token_major, keys_t


def _mix_proj(x, mod3, pre_g, w, wkt):
    B, S, D = x.shape
    tm = min(FFN_TILE, S)
    slopes = 2.0 ** -np.arange(1, NSA_HEADS + 1, dtype=np.float32)
    q_const = np.zeros((NSA_HEADS, NSA_AUG), np.float32)
    q_const[:, NSA_DK] = POS_SPLIT * slopes
    q_const[:, NSA_DK + 1] = slopes
    v_const = np.zeros((NSA_GROUPS, NSA_VAUG), np.float32)
    v_const[:, NSA_DK] = 1.0
    dtypes = (BF16, BF16, BF16, BF16, F32, F32, F32)
    tok = lambda n: pl.BlockSpec((None, tm, n), lambda b, i: (b, i, 0))
    return pl.pallas_call(
        _proj_kernel,
        grid=(B, S // tm),
        in_specs=[
            tok(D),
            pl.BlockSpec((None, 3, D), lambda b, i: (b, 0, 0)),
            _const_spec((1, D)),
            _const_spec((D, PROJ_COLS)),
            _const_spec((KT_ROWS, D)),
            _const_spec((1, Q_AUG_COLS)),
            _const_spec((1, V_AUG_COLS)),
        ],
        out_specs=[tok(n) for n in PROJ_WIDTHS] + [
            pl.BlockSpec((None, NSA_GROUPS, 1, NSA_AUG, tm), lambda b, i: (b, 0, i, 0, 0)),
            pl.BlockSpec((None, NSA_GROUPS, tm // LANES, NSA_AUG, LANES), lambda b, i: (b, 0, i, 0, 0))],
        out_shape=[jax.ShapeDtypeStruct((B, S, n), dt) for n, dt in zip(PROJ_WIDTHS, dtypes)] + [
            jax.ShapeDtypeStruct((B, NSA_GROUPS, S // tm, NSA_AUG, tm), BF16),
            jax.ShapeDtypeStruct((B, NSA_GROUPS, S // LANES, NSA_AUG, LANES), BF16)],
        compiler_params=_cparams("parallel", "parallel"),
        name="mix_proj",
    )(x, mod3, pre_g.reshape(1, D), w, wkt,
      jnp.asarray(q_const.reshape(1, -1)), jnp.asarray(v_const.reshape(1, -1)))


def _cmp_kernel(xk_ref, xv_ref, pek_ref, pev_ref, w1k_ref, w1v_ref, w2kt_ref, w2v_ref,
                kct_ref, vc_ref):
    half = xk_ref.shape[1]

    def hidden(x_ref, pe_ref, w1_ref):
        x = x_ref[...].astype(F32)
        n = x.shape[0]
        top = _dot(x + pe_ref[0:1, :], w1_ref[0:half, :])
        bot = _dot(x + pe_ref[1:2, :], w1_ref[half:2 * half, :])
        return _silu(top + pltpu.roll(bot, n - 1, axis=0))

    kct_ref[...] = _dot_nt(w2kt_ref[...], hidden(xk_ref, pek_ref, w1k_ref)).astype(BF16)
    vc_ref[...] = _dot(hidden(xv_ref, pev_ref, w1v_ref), w2v_ref[...]).astype(BF16)


def _compress(xk, xv, pe_k, pe_v, w1k, w1v, w2kt, w2v):
    B, NC, F = xk.shape
    G = NSA_GROUPS
    blk = lambda *s: pl.BlockSpec((None, None) + s, lambda b, g: (b, g, 0, 0))
    rows = pl.BlockSpec((None, NC, F), lambda b, g: (b, 0, 0))
    w1 = pl.BlockSpec((None, 2 * F, CMP_HID), lambda b, g: (g, 0, 0))
    return pl.pallas_call(
        _cmp_kernel,
        grid=(B, G),
        in_specs=[rows, rows, _const_spec((2, F)), _const_spec((2, F)), w1, w1,
                  _const_spec((NSA_DK, CMP_HID)), _const_spec((CMP_HID, NSA_VAUG))],
        out_specs=[blk(NSA_DK, NC), blk(NC, NSA_VAUG)],
        out_shape=[jax.ShapeDtypeStruct((B, G, NSA_DK, NC), BF16),
                   jax.ShapeDtypeStruct((B, G, NC, NSA_VAUG), BF16)],
        compiler_params=_cparams("parallel", "parallel"),
        name="nsa_compress",
    )(xk, xv, pe_k, pe_v, w1k, w1v, w2kt, w2v)


def _masked_softmax(s, valid):
    s = jnp.where(valid, s, NEG_INF)
    m = jnp.max(s, axis=-1, keepdims=True)
    p = jnp.where(valid, jnp.exp(s - m), 0.0)
    l = jnp.sum(p, axis=-1, keepdims=True)
    return p * (1.0 / jnp.where(l > 0.0, l, 1.0))


def _nsa_kernel(q_ref, gt_ref, kc_ref, vc_ref, ks_ref, vs_ref, kw_ref, vw_ref, selt_ref, e_ref,
                o_ref, imp_ref, tiles_ref, *, n_top, tk):
    s0 = pl.program_id(2) * NSA_STEP
    R = NSA_HPG * Q_BLOCK
    n_slc = selt_ref.shape[0]
    n_cmp = kc_ref.shape[-1]
    parts = range(NSA_STEP // Q_BLOCK)
    each = lambda fn, *lists: [fn(*xs) for xs in zip(*lists)]
    rows = lambda c: slice(c * Q_BLOCK, (c + 1) * Q_BLOCK)
    head_cols = lambda h: slice(h * NSA_AUG, (h + 1) * NSA_AUG)
    q = [jnp.concatenate([q_ref[rows(c), head_cols(h)] for h in range(NSA_HPG)], axis=0) for c in parts]
    q_row = lax.broadcasted_iota(jnp.int32, (R, 1), 0) & (Q_BLOCK - 1)
    t_i = [s0 + c * Q_BLOCK + q_row for c in parts]

    def normalised(acc):
        return acc * (1.0 / acc[:, NSA_DK:NSA_DK + 1])

    cmp_end = lax.broadcasted_iota(jnp.int32, (1, n_cmp), 1) * CMP_STRIDE + (CMP_LEN - 1)
    s_c = [jnp.dot(x, kc_ref[...], preferred_element_type=F32) for x in q]
    p_c = each(lambda s, t: _masked_softmax(s, cmp_end <= t), s_c, t_i)
    o_c = [_dot(p, vc_ref[...]) for p in p_c]

    j = lax.broadcasted_iota(jnp.int32, (n_slc, Q_BLOCK), 0)
    imp = []
    for c in parts:
        p_sum = p_c[c][0:Q_BLOCK]
        for h in range(1, NSA_HPG):
            p_sum = p_sum + p_c[c][h * Q_BLOCK:(h + 1) * Q_BLOCK]
        hi, lo = _split(p_sum)
        raw = _dot_nt(selt_ref[...], hi) + _dot_nt(selt_ref[...], lo)
        tq = s0 + c * Q_BLOCK + lax.broadcasted_iota(jnp.int32, (n_slc, Q_BLOCK), 1)
        cur = tq // SLC_BLOCK
        forced = (j == 0) | (j == cur) | (j == cur - 1)
        live = j * SLC_BLOCK <= tq
        imp.append(jnp.where(forced, FORCE_SCORE, jnp.where(live, raw, NEG_INF)))
        imp_ref[c] = imp[c]

    def rank_body(i, cnt):
        tie = jnp.where(j > i, 1.0, 0.0)
        out = []
        for c in parts:
            r = imp_ref[c, pl.ds(i, 1), :]
            out.append(cnt[c] + jnp.where(r > imp[c], 1.0, jnp.where(r == imp[c], tie, 0.0)))
        return tuple(out)

    n_live = (s0 + NSA_STEP) // SLC_BLOCK
    cnt = lax.fori_loop(0, n_live, rank_body, tuple(jnp.zeros((n_slc, Q_BLOCK), F32) for _ in parts))
    lhs = []
    for c in parts:
        dropped = jnp.concatenate(
            [jnp.where(cnt[c] < n_top, 0.0, 1.0), jnp.zeros((NSA_AUG - n_slc, Q_BLOCK), F32)], axis=0)
        dropped = dropped.T.astype(BF16)
        lhs.append(jnp.concatenate([q[c], jnp.concatenate([dropped] * NSA_HPG, axis=0)], axis=1))

    last = (s0 + NSA_STEP - 1) // tk
    chosen = jnp.where(cnt[0] < n_top, 1.0, 0.0)
    for c in parts[1:]:
        chosen = jnp.maximum(chosen, jnp.where(cnt[c] < n_top, 1.0, 0.0))
    blocks_per_tile = tk // SLC_BLOCK
    n_used = jnp.int32(0)
    for kt in range(n_slc // blocks_per_tile):
        used = jnp.max(chosen[kt * blocks_per_tile:(kt + 1) * blocks_per_tile, :]) > 0.0
        tiles_ref[n_used] = kt
        n_used = n_used + jnp.where(used & (kt < last), 1, 0).astype(jnp.int32)

    def sel_tile(kt, carry, diagonal):
        m, acc = carry[0::2], carry[1::2]
        rhs = jnp.concatenate([ks_ref[kt], e_ref[kt]], axis=0)
        v = vs_ref[pl.ds(pl.multiple_of(kt * tk, tk), tk), :]
        s = [jnp.dot(x, rhs, preferred_element_type=F32) for x in lhs]
        if diagonal:
            pos = kt * tk + lax.broadcasted_iota(jnp.int32, (1, tk), 1)
            s = each(lambda x, t: jnp.where(pos <= t, x, NEG_INF), s, t_i)
        m_new = each(lambda x, y: jnp.maximum(x, jnp.max(y, axis=-1, keepdims=True)), m, s)
        alpha = each(lambda x, y: jnp.exp(x - y), m, m_new)
        p = each(lambda x, y: jnp.exp(x - y).astype(BF16), s, m_new)
        acc = each(lambda a, x, y: a * x + jnp.dot(y, v, preferred_element_type=F32), alpha, acc, p)
        out = []
        for x, y in zip(m_new, acc):
            out += [x, y]
        return tuple(out)

    init = []
    for _ in parts:
        init += [jnp.full((R, 1), NEG_INF, F32), jnp.zeros((R, NSA_VAUG), F32)]
    carry = lax.fori_loop(0, n_used, lambda i, x: sel_tile(tiles_ref[i], x, False), tuple(init))
    o_s = [normalised(x) for x in sel_tile(last, carry, True)[1::2]]

    n_wt = (WINDOW + Q_BLOCK) // LANES
    lane_pos = lax.broadcasted_iota(jnp.int32, (1, LANES), 1)
    s_w, start = [], []
    for c in parts:
        start.append(jnp.maximum(s0 + c * Q_BLOCK - WINDOW, 0))
        tiles = []
        for i in range(n_wt):
            pos = start[c] + i * LANES + lane_pos
            x = jnp.dot(q[c], kw_ref[start[c] // LANES + i], preferred_element_type=F32)
            x = jnp.where(pos <= t_i[c], x, NEG_INF)
            if i == 0:
                x = jnp.where(pos > t_i[c] - WINDOW, x, NEG_INF)
            tiles.append(x)
        s_w.append(jnp.concatenate(tiles, axis=1))
    p_w = [jnp.exp(x - jnp.max(x, axis=-1, keepdims=True)) for x in s_w]
    o_w = [normalised(_dot(p, vw_ref[pl.ds(pl.multiple_of(st, LANES), n_wt * LANES), :]))
           for p, st in zip(p_w, start)]

    for c in parts:
        gate = _sigmoid(gt_ref[rows(c), :])
        for h in range(NSA_HPG):
            hr = slice(h * Q_BLOCK, (h + 1) * Q_BLOCK)
            o = (gate[:, 3 * h:3 * h + 1] * o_c[c][hr] + gate[:, 3 * h + 1:3 * h + 2] * o_s[c][hr]
                 + gate[:, 3 * h + 2:3 * h + 3] * o_w[c][hr])
            o_ref[rows(c), head_cols(h)] = o.astype(o_ref.dtype)


def _selection_map_t(n_cmp_rows, n_slc):
    rs, rc = SLC_BLOCK // CMP_STRIDE, CMP_LEN // CMP_STRIDE
    m = np.zeros((n_slc, n_cmp_rows), np.float32)
    for jj in range(n_slc):
        for a in range(rs):
            for b in range(rc):
                i = rs * jj - a - b
                if 0 <= i < n_cmp_rows - 1:
                    m[jj, i] += 1.0
    return jnp.asarray(m, BF16)


def _nsa_attention(q, gt, kc, vc, ks, vs, kw, vw):
    B, S, _ = q.shape
    G, A = NSA_GROUPS, NSA_AUG
    n_slc = S // SLC_BLOCK
    n_cmp = kc.shape[-1]
    n_kt, _, tk = ks.shape[2:]
    selt = _selection_map_t(n_cmp, n_slc)
    key_blk = np.arange(S).reshape(n_kt, 1, tk) // SLC_BLOCK
    e = np.where(np.arange(A).reshape(1, A, 1) == key_blk, -MASK_BIG, 0.0)
    heads = pl.BlockSpec((None, NSA_STEP, NSA_HPG * A), lambda b, g, i: (b, i, g))
    values = pl.BlockSpec((None, S, NSA_VAUG), lambda b, g, i: (b, 0, g))
    per_g = lambda *s: pl.BlockSpec((None, None) + s, lambda b, g, i: (b, g) + (0,) * len(s))
    kern = functools.partial(_nsa_kernel, n_top=min(SLC_TOPN, n_slc), tk=tk)
    return pl.pallas_call(
        kern,
        grid=(B, G, S // NSA_STEP),
        in_specs=[heads, pl.BlockSpec((None, NSA_STEP, LANES), lambda b, g, i: (b, i, g)),
                  per_g(A, n_cmp), per_g(n_cmp, NSA_VAUG),
                  per_g(n_kt, A, tk), values, per_g(S // LANES, A, LANES), values,
                  pl.BlockSpec((n_slc, n_cmp), lambda b, g, i: (0, 0)),
                  pl.BlockSpec((n_kt, A, tk), lambda b, g, i: (0, 0, 0))],
        out_specs=heads,
        out_shape=jax.ShapeDtypeStruct((B, S, NSA_HEADS * NSA_VAUG), BF16),
        scratch_shapes=[pltpu.VMEM((NSA_STEP // Q_BLOCK, n_slc, Q_BLOCK), F32),
                        pltpu.SMEM((n_kt,), jnp.int32)],
        compiler_params=_cparams("parallel", "parallel", "arbitrary"),
        name="nsa_attention",
    )(q, gt, kc, vc, ks, vs, kw, vw, selt, jnp.asarray(e, BF16))


def _with_position_rows(kt, pos):
    lead = kt.shape[:-2]
    n = kt.shape[-1]
    hi = jnp.broadcast_to(jnp.asarray(pos // POS_SPLIT, BF16)[..., None, :], lead + (1, n))
    lo = jnp.broadcast_to(jnp.asarray(pos % POS_SPLIT, BF16)[..., None, :], lead + (1, n))
    pad = jnp.zeros(lead + (NSA_AUG - NSA_DK - 2, n), BF16)
    return jnp.concatenate([kt, hi, lo, pad], axis=-2)


def _nsa_mixer(q, kvc, vs, vw, gt, ks, kw, k_w1, k_w2, k_pe, v_w1, v_w2, v_pe):
    B, S, _ = q.shape
    G, DK = NSA_GROUPS, NSA_DK
    half = CMP_LEN // 2
    nc = S // half

    def half_blocks(z):
        return z.reshape(B, nc, half * G * DK)

    def both_groups(pe):
        return jnp.broadcast_to(pe[:, None, :], (CMP_LEN, G, DK)).reshape(2, half * G * DK)

    def per_group(w1):
        w1 = w1.reshape(CMP_LEN, DK, CMP_HID)
        return jnp.stack([jnp.zeros((CMP_LEN, G, DK, CMP_HID), F32).at[:, g].set(w1)
                          .reshape(CMP_LEN * G * DK, CMP_HID) for g in range(G)]).astype(BF16)

    kct, vcc = _compress(
        half_blocks(kvc[:, :, :G * DK]), half_blocks(kvc[:, :, G * DK:]),
        both_groups(k_pe), both_groups(v_pe),
        per_group(k_w1), per_group(v_w1), k_w2.T.astype(BF16),
        jnp.pad(v_w2, ((0, 0), (0, NSA_VAUG - DK))).astype(BF16))
    kca = _with_position_rows(kct, np.arange(nc) * CMP_STRIDE + (CMP_LEN - 1))
    return _nsa_attention(q, gt, kca, vcc, ks, vs, kw, vw)


def _gelu_tanh(x):
    return 0.5 * x * (1.0 + jnp.tanh(np.sqrt(2.0 / np.pi).astype(np.float32) * (x + 0.044715 * (x * x * x))))


def _sgu_kernel(p_ref, g_ref, b_ref, w_ref, bs_ref, o_ref):
    ts = p_ref.shape[0]
    gd = SGU_WIDTH // SGU_GROUPS
    u = _gelu_tanh(p_ref[:, 0:SGU_WIDTH])
    v = _gelu_tanh(p_ref[:, SGU_WIDTH:2 * SGU_WIDTH])
    mu = jnp.mean(v, axis=-1, keepdims=True)
    vc = v - mu
    var = jnp.mean(vc * vc, axis=-1, keepdims=True)
    vn = (vc * lax.rsqrt(var + LN_EPS) * g_ref[...] + b_ref[...]).astype(BF16)
    causal = (lax.broadcasted_iota(jnp.int32, (SGU_CHUNK, SGU_CHUNK), 0)
              >= lax.broadcasted_iota(jnp.int32, (SGU_CHUNK, SGU_CHUNK), 1))
    ws = [jnp.where(causal, w_ref[i], 0.0).astype(BF16) for i in range(SGU_GROUPS)]
    for n in range(ts // SGU_CHUNK):
        r0, r1 = n * SGU_CHUNK, (n + 1) * SGU_CHUNK
        s = jnp.concatenate(
            [jnp.dot(ws[i], vn[r0:r1, i * gd:(i + 1) * gd], preferred_element_type=F32)
             for i in range(SGU_GROUPS)], axis=1) + bs_ref[...]
        o_ref[r0:r1, :] = (u[r0:r1] * s).astype(o_ref.dtype)


def _sgu_mixer(p, ln_g, ln_b, w_s, b_s):
    B, S, _ = p.shape
    ts = min(SGU_TILE, S)
    gd = SGU_WIDTH // SGU_GROUPS
    bias = jnp.repeat(b_s.T, gd, axis=1)
    return pl.pallas_call(
        _sgu_kernel,
        grid=(B, S // ts),
        in_specs=[pl.BlockSpec((None, ts, 2 * SGU_WIDTH), lambda b, i: (b, i, 0)),
                  _const_spec((1, SGU_WIDTH)), _const_spec((1, SGU_WIDTH)),
                  _const_spec((SGU_GROUPS, SGU_CHUNK, SGU_CHUNK)), _const_spec((SGU_CHUNK, SGU_WIDTH))],
        out_specs=pl.BlockSpec((None, ts, SGU_WIDTH), lambda b, i: (b, i, 0)),
        out_shape=jax.ShapeDtypeStruct((B, S, SGU_WIDTH), BF16),
        compiler_params=_cparams("parallel", "parallel"),
        name="sgu",
    )(p, ln_g.reshape(1, -1), ln_b.reshape(1, -1), w_s, bias)


def _head_sum(x, bd_ref):
    return _dot_x2(x, bd_ref[...])


def _rwkv_operands(p_ref, carry, mu_ref, w0_ref, w2_ref, a0_ref, a2_ref, g2_ref, kk_ref, ka_ref, bd_ref,
                   v_res_refs):
    W = RWKV_WIDTH
    p = p_ref[...]
    tr = p.shape[0]

    @pl.when(pl.program_id(1) == 0)
    def _():
        carry[...] = jnp.zeros_like(carry)

    rowi = lax.broadcasted_iota(jnp.int32, (tr, 1), 0)
    p_prev = jnp.where(rowi == 0, carry[0:1, :], pltpu.roll(p, 1, axis=0))
    carry[0:1, :] = p[tr - 1:tr, :]
    ps = p + (p_prev - p) * mu_ref[...]
    r, k, v = ps[:, 0:W], ps[:, W:2 * W], ps[:, 2 * W:3 * W]
    wa = ps[:, 3 * W:3 * W + DECAY_LORA + AAA_LORA]
    gd = ps[:, 3 * W + DECAY_LORA + AAA_LORA:]
    w = w0_ref[...] + _dot(jnp.tanh(wa), w2_ref[...])
    x = -w
    softplus = jnp.maximum(x, 0.0) + jnp.log(1.0 + jnp.exp(-jnp.abs(x)))
    ld = -jnp.exp(-softplus - 0.5)
    a = _sigmoid(a0_ref[...] + _dot(wa, a2_ref[...]))
    g = _dot(_sigmoid(gd), g2_ref[...])
    if v_res_refs is not None:
        vf_ref, v0_ref, v1_ref, v2_ref = v_res_refs
        lora = _dot(_dot(v, v1_ref[...]), v2_ref[...])
        v = v + (vf_ref[...] - v) * _sigmoid(v0_ref[...] + lora)
    kk = k * kk_ref[...]
    norm = jnp.sqrt(_head_sum(kk * kk, bd_ref))
    kn = kk * (1.0 / jnp.maximum(norm, 1e-12))
    return r, ld, k * (1.0 + (a - 1.0) * ka_ref[...]), v, kn, kn * a, g


def _head_block_diag():
    idx = np.arange(RWKV_WIDTH) // RWKV_HEAD_DIM
    return jnp.asarray(idx[:, None] == idx[None, :], BF16)


def _rwkv_kernel(*refs, has_vres):
    (p_ref, mu_ref, w0_ref, w2_ref, a0_ref, a2_ref, g2_ref, kk_ref, ka_ref,
     rk_ref, lg_ref, lb_ref, bd_ref) = refs[:13]
    if has_vres:
        v_res_refs = refs[13:17]
        o_ref, state, carry = refs[17:]
    else:
        v_res_refs = None
        o_ref, v_ref, state, carry = refs[13:]
    r, ld, k, v, kn, ba, g = _rwkv_operands(p_ref, carry, mu_ref, w0_ref, w2_ref, a0_ref, a2_ref, g2_ref,
                                            kk_ref, ka_ref, bd_ref, v_res_refs)
    if not has_vres:
        v_ref[...] = v
    L = RWKV_CHUNK
    N = RWKV_HEAD_DIM
    NH = RWKV_PACK
    QW = NH * N
    n_sub = r.shape[0] // L

    @pl.when(pl.program_id(1) == 0)
    def _():
        state[...] = jnp.zeros_like(state)

    tt = lax.broadcasted_iota(jnp.int32, (n_sub * L, n_sub * L), 0)
    ts = lax.broadcasted_iota(jnp.int32, (n_sub * L, n_sub * L), 1)
    tril_ones = jnp.where((tt >= ts) & (tt // L == ts // L), 1.0, 0.0).astype(BF16)
    own = (lax.broadcasted_iota(jnp.int32, (QW, QW), 0) // N
           == lax.broadcasted_iota(jnp.int32, (QW, QW), 1) // N)
    ti = lax.broadcasted_iota(jnp.int32, (L, QW), 0)
    si = lax.broadcasted_iota(jnp.int32, (L, QW), 1) % L
    strict = ti > si
    incl = ti >= si
    eye_tok = jnp.where(ti == si, 1.0, 0.0)
    eye_ch = jnp.where(lax.broadcasted_iota(jnp.int32, (QW, QW), 0)
                       == lax.broadcasted_iota(jnp.int32, (QW, QW), 1), 1.0, 0.0)

    def stacked(x):
        x = x.astype(BF16)
        return jnp.where(own, jnp.concatenate([x] * NH, axis=0), jnp.zeros((), BF16))

    h1 = ld.astype(BF16)
    r1 = ld - h1.astype(F32)
    h2 = r1.astype(BF16)
    h3 = (r1 - h2.astype(F32)).astype(BF16)
    cum = _dot(tril_ones, h1) + (_dot(tril_ones, h2) + _dot(tril_ones, h3))
    w_in = jnp.exp(cum)
    w_inv = jnp.exp(-cum)
    rt = r * w_in
    kt = k * w_inv
    at = -kn * jnp.exp(cum - ld)
    bt = ba * w_inv

    n_grp = RWKV_WIDTH // QW
    chains = [(c, q) for c in range(n_sub) for q in range(n_grp)]
    each = lambda fn, *lists: [fn(*xs) for xs in zip(*lists)]
    pieces = lambda z: [z[c * L:(c + 1) * L, q * QW:(q + 1) * QW] for c, q in chains]
    cat0 = lambda *xs: jnp.concatenate(xs, axis=0)
    cat1 = lambda *xs: jnp.concatenate(xs, axis=1)
    a_t, r_t, b_t, k_t, v_t = pieces(at), pieces(rt), pieces(bt), pieces(kt), pieces(v)
    v_st = [stacked(x) for x in v_t]
    gram = each(lambda a, r, b, k: _dot_nt(cat0(a, r), cat0(stacked(b), stacked(k))), a_t, r_t, b_t, k_t)
    a_ab = [jnp.where(strict, x[0:L, 0:QW], 0.0) for x in gram]
    a_ak = [jnp.where(strict, x[0:L, QW:2 * QW], 0.0) for x in gram]
    m_rb = [jnp.where(incl, x[L:2 * L, 0:QW], 0.0) for x in gram]
    m_rk = [jnp.where(incl, x[L:2 * L, QW:2 * QW], 0.0) for x in gram]
    t_inv = [eye_tok + x for x in a_ab]
    power = each(lambda p: _dot(p, stacked(p)), a_ab)
    for _ in range(int(np.log2(L)) - 2):
        both = each(lambda p, t: _dot(cat0(p, t), stacked(p)), power, t_inv)
        power = [x[0:L] for x in both]
        t_inv = each(lambda t, x: t + x[L:2 * L], t_inv, both)
    t_inv = each(lambda t, p: t + _dot(t, stacked(p)), t_inv, power)
    akv = each(_dot, a_ak, v_st)
    au = each(lambda t, a, x: _dot(t, cat1(stacked(a), stacked(x))), t_inv, a_t, akv)
    ahat = [x[:, 0:QW] for x in au]
    uhat = [x[:, QW:2 * QW] for x in au]
    rhat = each(lambda r, m, a: r + _dot(m, stacked(a)), r_t, m_rb, ahat)
    y0 = each(lambda mb, mk, u, x: _dot(cat1(mb, mk), cat0(stacked(u), x)), m_rb, m_rk, uhat, v_st)
    w_last = [w_in[(c + 1) * L - 1:(c + 1) * L, q * QW:(q + 1) * QW] for c, q in chains]
    p_mat = each(lambda a, b, w: jnp.where(own, eye_ch + _dot_tn(a, b), 0.0) * w, ahat, b_t, w_last)
    q_mat = each(lambda u, x, b, k, w: jnp.where(own, _dot_tn(cat0(u, x), cat0(b, k)), 0.0) * w,
                 uhat, v_t, b_t, k_t, w_last)

    y_chunks = []
    for c in range(n_sub):
        idx = range(c * n_grp, (c + 1) * n_grp)
        s_old = [state[q] for q in range(n_grp)]
        y_chunks.append(jnp.concatenate([_dot_nt(rhat[i], s) + y0[i] for i, s in zip(idx, s_old)], axis=1))
        for q, i in enumerate(idx):
            state[q] = _dot(s_old[q], p_mat[i]) + q_mat[i]
    y = jnp.concatenate(y_chunks, axis=0)
    inv_n = 1.0 / N
    mean = _head_sum(y, bd_ref) * inv_n
    yc = y - mean
    var = _head_sum(yc * yc, bd_ref) * inv_n
    yn = yc * lax.rsqrt(var + LNX_EPS) * lg_ref[...] + lb_ref[...]
    bonus = _head_sum(r * k * rk_ref[...], bd_ref) * v
    o_ref[...] = ((yn + bonus) * g).astype(o_ref.dtype)


def _rwkv_mixer(p, mu, w0, w2, a0, a2, g2, k_k, k_a, r_k, lnx_g, lnx_b, v_first, v_res):
    B, S, C = p.shape
    W = RWKV_WIDTH
    step = min(RWKV_STEP, S)
    qw = RWKV_PACK * RWKV_HEAD_DIM
    row = lambda z: z.reshape(1, -1)
    lora_rows = DECAY_LORA + AAA_LORA
    w2p = jnp.zeros((lora_rows, W), F32).at[:DECAY_LORA].set(w2).astype(BF16)
    a2p = jnp.zeros((lora_rows, W), F32).at[DECAY_LORA:].set(a2).astype(BF16)
    tok = lambda n: pl.BlockSpec((None, step, n), lambda b, i: (b, i, 0))
    vec = _const_spec((1, W))
    args = [p, row(mu), row(w0), w2p, row(a0), a2p, g2.astype(BF16), row(k_k), row(k_a),
            row(r_k), row(lnx_g), row(lnx_b), _head_block_diag()]
    specs = [tok(C), _const_spec((1, C)), vec, _const_spec((lora_rows, W)), vec,
             _const_spec((lora_rows, W)), _const_spec((GATE_LORA, W)), vec, vec,
             vec, vec, vec, _const_spec((W, W))]
    out_specs = [tok(W)]
    out_shape = [jax.ShapeDtypeStruct((B, S, W), BF16)]
    if v_res is None:
        out_specs.append(tok(W))
        out_shape.append(jax.ShapeDtypeStruct((B, S, W), F32))
    else:
        v0, v1, v2 = v_res
        args += [v_first, row(v0), v1.astype(BF16), v2.astype(BF16)]
        specs += [tok(W), vec, _const_spec((W, MV_LORA)), _const_spec((MV_LORA, W))]
    out = pl.pallas_call(
        functools.partial(_rwkv_kernel, has_vres=v_res is not None),
        grid=(B, S // step),
        in_specs=specs,
        out_specs=out_specs,
        out_shape=out_shape,
        scratch_shapes=[pltpu.VMEM((W // qw, qw, qw), F32), pltpu.VMEM((8, C), F32)],
        compiler_params=_cparams("parallel", "arbitrary"),
        name="rwkv",
    )(*args)
    return out[0], (out[1] if v_res is None else v_first)


def _merge_kernel(x_ref, mod_ref, pg_ref, qg_ref, ya_ref, yb_ref, yc_ref, wg_ref,
                  wa_ref, wb_ref, wc_ref, ow_ref, o_ref):
    x = x_ref[...]
    D = x.shape[1]
    h = _pre(x, pg_ref[...], mod_ref).astype(BF16)
    merged = None
    for i, (y_ref, w_ref) in enumerate(((ya_ref, wa_ref), (yb_ref, wb_ref), (yc_ref, wc_ref))):
        gate = _sigmoid(jnp.dot(h, wg_ref[:, i * D:(i + 1) * D], preferred_element_type=F32))
        term = gate * jnp.dot(y_ref[...], w_ref[...], preferred_element_type=F32)
        merged = term if merged is None else merged + term
    y = _dot(merged, ow_ref[...])
    o_ref[...] = x + mod_ref[2:3, :] * _rms(y, qg_ref[...])


def _merge(x, mod3, pre_g, post_g, y_a, y_b, y_c, w_gate, branch_w, out_w):
    B, S, D = x.shape
    tm = min(FFN_TILE, S)
    tok = lambda n: pl.BlockSpec((None, tm, n), lambda b, i: (b, i, 0))
    w_a = jnp.pad(branch_w[0].reshape(NSA_HEADS, NSA_DK, D), ((0, 0), (0, NSA_VAUG - NSA_DK), (0, 0)))
    w_a = w_a.reshape(NSA_HEADS * NSA_VAUG, D)
    widths = (y_a.shape[-1], y_b.shape[-1], y_c.shape[-1])
    return pl.pallas_call(
        _merge_kernel,
        grid=(B, S // tm),
        in_specs=[tok(D), pl.BlockSpec((None, 3, D), lambda b, i: (b, 0, 0)),
                  _const_spec((1, D)), _const_spec((1, D))]
                 + [tok(n) for n in widths]
                 + [_const_spec((D, N_BRANCH * D))] + [_const_spec((n, D)) for n in widths]
                 + [_const_spec((D, D))],
        out_specs=tok(D),
        out_shape=jax.ShapeDtypeStruct((B, S, D), F32),
        compiler_params=_cparams("parallel", "parallel"),
        name="merge",
    )(x, mod3, pre_g.reshape(1, D), post_g.reshape(1, D), y_a, y_b, y_c, w_gate,
      w_a, branch_w[1], branch_w[2], out_w)


def kernel(x, c, ada_w, ada_b, pre_g, post_g, ffn_w_in, ffn_w_out, mix_w_in, branch_w, out_w,
           cmp_k_w1, cmp_k_w2, cmp_k_pe, cmp_v_w1, cmp_v_w2, cmp_v_pe,
           sgu_ln_g, sgu_ln_b, sgu_w, sgu_b,
           rwkv_mu, rwkv_w0, rwkv_w2, rwkv_a0, rwkv_a2, rwkv_g2, rwkv_kk, rwkv_ka, rwkv_rk,
           rwkv_lnx_g, rwkv_lnx_b, rwkv_v0, rwkv_v1, rwkv_v2):
    B, S, D = x.shape
    depth = ada_w.shape[0]
    mod = _ada_mod(c, ada_w, ada_b).reshape(depth, B, 3, 3, D)
    v_first = None
    for l in range(depth):
        x = _ffn(x, mod[l, :, 0], pre_g[l, 0], post_g[l, 0],
                 ffn_w_in[l, 0].astype(BF16), ffn_w_out[l, 0].astype(BF16))

        w = mix_w_in[l]
        n_in = NSA_COLS + 2 * SGU_WIDTH + RWKV_COLS
        q, kvc, vs, vw, gt, p_sgu, p_rwkv, ks, kw = _mix_proj(
            x, mod[l, :, 1], pre_g[l, 1], *_proj_weights(w))
        y_a = _nsa_mixer(q, kvc, vs, vw, gt, ks, kw, cmp_k_w1[l], cmp_k_w2[l], cmp_k_pe[l],
                         cmp_v_w1[l], cmp_v_w2[l], cmp_v_pe[l])
        y_b = _sgu_mixer(p_sgu, sgu_ln_g[l], sgu_ln_b[l], sgu_w[l], sgu_b[l])
        v_res = None if l == 0 else (rwkv_v0[l - 1], rwkv_v1[l - 1], rwkv_v2[l - 1])
        y_c, v_first = _rwkv_mixer(p_rwkv, rwkv_mu[l], rwkv_w0[l], rwkv_w2[l], rwkv_a0[l], rwkv_a2[l],
                                   rwkv_g2[l], rwkv_kk[l], rwkv_ka[l], rwkv_rk[l],
                                   rwkv_lnx_g[l], rwkv_lnx_b[l], v_first, v_res)
        x = _merge(x, mod[l, :, 1], pre_g[l, 1], post_g[l, 1], y_a, y_b, y_c,
                   w[:, n_in:].astype(BF16), branch_w[l].astype(BF16), out_w[l].astype(BF16))

        x = _ffn(x, mod[l, :, 2], pre_g[l, 2], post_g[l, 2],
                 ffn_w_in[l, 1].astype(BF16), ffn_w_out[l, 1].astype(BF16))
    return x
```

```python
import functools

import numpy as np
import jax
import jax.numpy as jnp
from jax import lax
from jax.experimental import pallas as pl
from jax.experimental.pallas import tpu as pltpu

D_MODEL = 1024
DEPTH = 2
NSA_HEADS = 8
NSA_GROUPS = 2
NSA_HPG = NSA_HEADS // NSA_GROUPS
NSA_DK = 64
CMP_LEN = 32
CMP_STRIDE = 16
CMP_HID = 256
SLC_BLOCK = 64
SLC_TOPN = 16
WINDOW = 512
Q_BLOCK = 128
SGU_CHUNK = 128
SGU_GROUPS = 4
SGU_WIDTH = 512
RWKV_HEADS = 8
RWKV_HEAD_DIM = 64
RWKV_WIDTH = RWKV_HEADS * RWKV_HEAD_DIM
DECAY_LORA = 64
AAA_LORA = 64
MV_LORA = 32
GATE_LORA = 128
N_BRANCH = 3
BRANCH_WIDTH = 512
D_FF = 2816
MACARON_WEIGHT = 0.5
RMS_EPS = 1e-6
LN_EPS = 1e-5
LNX_EPS = 64e-5
NEG_INF = -1e30
FORCE_SCORE = 1e4

NSA_COLS = NSA_HEADS * NSA_DK + 6 * NSA_GROUPS * NSA_DK + 3 * NSA_HEADS
RWKV_COLS = 3 * RWKV_WIDTH + DECAY_LORA + AAA_LORA + GATE_LORA
Q_COLS = NSA_HEADS * NSA_DK
KV_COLS = 6 * NSA_GROUPS * NSA_DK
GT_COLS = 3 * NSA_HEADS
LANES = 128
NSA_STEP = 256
NSA_VAUG = 128
NSA_AUG = 128
LOG2E = 1.4426950408889634
SLOPE_PIECES = 3
POS_SPLIT = 64
MASK_BIG = 2.0 ** 100

BF16 = jnp.bfloat16
F32 = jnp.float32

FFN_TILE = 512
FFN_CHUNK = 1408
SEL_TILE = 512
RWKV_CHUNK = 64
RWKV_PACK = 4
RWKV_STEP = 256
SGU_TILE = 512
VMEM_LIMIT = 56 * 1024 * 1024


def _cparams(*sem):
    return pltpu.CompilerParams(dimension_semantics=sem, vmem_limit_bytes=VMEM_LIMIT)


def _dot(a, b):
    return jnp.dot(a.astype(BF16), b.astype(BF16), preferred_element_type=F32)


def _dot_nt(a, b):
    return lax.dot_general(a.astype(BF16), b.astype(BF16), (((1,), (1,)), ((), ())),
                           preferred_element_type=F32)


def _dot_tn(a, b):
    return lax.dot_general(a.astype(BF16), b.astype(BF16), (((0,), (0,)), ((), ())),
                           preferred_element_type=F32)


def _split(x):
    hi = x.astype(BF16)
    lo = (x - hi.astype(F32)).astype(BF16)
    return hi, lo


def _dot_x2(a, b):
    hi, lo = _split(a)
    return _dot(hi, b) + _dot(lo, b)


def _sigmoid(x):
    return 1.0 / (1.0 + jnp.exp(-x))


def _silu(x):
    return x * _sigmoid(x)


def _rms(x, g):
    return x * lax.rsqrt(jnp.mean(x * x, axis=-1, keepdims=True) + RMS_EPS) * g


def _pre(x, g, mod_ref):
    return _rms(x, g) * (1.0 + mod_ref[1:2, :]) + mod_ref[0:1, :]


def _ada_kernel(c_ref, w_ref, b_ref, o_ref):
    cond = _silu(c_ref[...])
    o_ref[...] = _dot(cond, w_ref[...]) + b_ref[...]


def _ada_mod(c, ada_w, ada_b):
    L, D, N = ada_w.shape
    B = c.shape[0]
    tn = 1536
    return pl.pallas_call(
        _ada_kernel,
        grid=(L, N // tn),
        in_specs=[
            pl.BlockSpec((B, D), lambda l, j: (0, 0)),
            pl.BlockSpec((None, D, tn), lambda l, j: (l, 0, j)),
            pl.BlockSpec((None, 1, tn), lambda l, j: (l, 0, j)),
        ],
        out_specs=pl.BlockSpec((None, B, tn), lambda l, j: (l, 0, j)),
        out_shape=jax.ShapeDtypeStruct((L, B, N), F32),
        compiler_params=_cparams("arbitrary", "arbitrary"),
        name="ada_mod",
    )(c, ada_w, ada_b.reshape(L, 1, N))


def _ffn_kernel(x_ref, mod_ref, pg_ref, qg_ref, win_ref, wout_ref, o_ref):
    x = x_ref[...]
    h = _pre(x, pg_ref[...], mod_ref).astype(BF16)
    acc = None
    for c in range(D_FF // FFN_CHUNK):
        lo, hi = c * FFN_CHUNK, (c + 1) * FFN_CHUNK
        gate = jnp.dot(h, win_ref[:, lo:hi], preferred_element_type=F32)
        up = jnp.dot(h, win_ref[:, D_FF + lo:D_FF + hi], preferred_element_type=F32)
        act = (_silu(gate) * up).astype(BF16)
        part = jnp.dot(act, wout_ref[lo:hi, :], preferred_element_type=F32)
        acc = part if acc is None else acc + part
    o_ref[...] = x + MACARON_WEIGHT * mod_ref[2:3, :] * _rms(acc, qg_ref[...])


def _const_spec(shape):
    nd = len(shape)
    return pl.BlockSpec(shape, lambda *_: (0,) * nd, pipeline_mode=pl.Buffered(1))


def _ffn(x, mod3, pre_g, post_g, w_in, w_out):
    B, S, D = x.shape
    tm = min(FFN_TILE, S)
    return pl.pallas_call(
        _ffn_kernel,
        grid=(B, S // tm),
        in_specs=[
            pl.BlockSpec((None, tm, D), lambda b, i: (b, i, 0)),
            pl.BlockSpec((None, 3, D), lambda b, i: (b, 0, 0)),
            _const_spec((1, D)),
            _const_spec((1, D)),
            _const_spec((D, 2 * D_FF)),
            _const_spec((D_FF, D)),
        ],
        out_specs=pl.BlockSpec((None, tm, D), lambda b, i: (b, i, 0)),
        out_shape=jax.ShapeDtypeStruct((B, S, D), F32),
        compiler_params=_cparams("parallel", "parallel"),
        name="ffn",
    )(x, mod3, pre_g.reshape(1, D), post_g.reshape(1, D), w_in, w_out)


Q_AUG_COLS = NSA_HEADS * NSA_AUG
V_AUG_COLS = NSA_GROUPS * NSA_VAUG
CMP_COLS = 2 * NSA_GROUPS * NSA_DK
GT_AUG_COLS = NSA_GROUPS * LANES
PROJ_WIDTHS = (Q_AUG_COLS, CMP_COLS, V_AUG_COLS, V_AUG_COLS, GT_AUG_COLS, 2 * SGU_WIDTH, RWKV_COLS)
PROJ_COLS = sum(PROJ_WIDTHS)
KT_ROWS = 2 * NSA_GROUPS * NSA_AUG


def _proj_kernel(x_ref, mod_ref, pg_ref, w_ref, wkt_ref, qc_ref, vc_ref,
                 q_ref, cmp_ref, vs_ref, vw_ref, gt_ref, sgu_ref, rw_ref, ks_ref, kw_ref):
    h = _pre(x_ref[...], pg_ref[...], mod_ref).astype(BF16)
    tm = h.shape[0]
    edges = np.cumsum((0,) + PROJ_WIDTHS)
    cols = [jnp.dot(h, w_ref[:, int(a):int(b)], preferred_element_type=F32)
            for a, b in zip(edges[:-1], edges[1:])]
    q_ref[...] = (cols[0] * (NSA_DK ** -0.5 * LOG2E) + qc_ref[...]).astype(BF16)
    cmp_ref[...] = cols[1].astype(BF16)
    vs_ref[...] = (cols[2] + vc_ref[...]).astype(BF16)
    vw_ref[...] = (cols[3] + vc_ref[...]).astype(BF16)
    gt_ref[...] = cols[4]
    sgu_ref[...] = cols[5]
    rw_ref[...] = cols[6]
    kt = _dot_nt(wkt_ref[...], h)
    pos = pl.program_id(1) * tm + lax.broadcasted_iota(jnp.int32, (1, tm), 1)
    row = lax.broadcasted_iota(jnp.int32, (NSA_AUG, 1), 0)
    in_rows = (row >= NSA_DK) & (row < NSA_DK + 2 * SLOPE_PIECES)
    pos_rows = jnp.where(in_rows, jnp.where((row - NSA_DK) % 2 == 0, (pos // POS_SPLIT).astype(F32),
                                            (pos % POS_SPLIT).astype(F32)), 0.0)
    for g in range(NSA_GROUPS):
        ks_ref[g, 0] = (kt[g * NSA_AUG:(g + 1) * NSA_AUG] + pos_rows).astype(BF16)
        kw = (kt[(NSA_GROUPS + g) * NSA_AUG:(NSA_GROUPS + g + 1) * NSA_AUG] + pos_rows).astype(BF16)
        for t in range(tm // LANES):
            kw_ref[g, t] = kw[:, t * LANES:(t + 1) * LANES]


def _proj_weights(w):
    D = w.shape[0]
    G, DK = NSA_GROUPS, NSA_DK
    q, kc, vc, ks, vs, kw, vw = [w[:, a:b] for a, b in zip(
        np.cumsum((0, Q_COLS) + (G * DK,) * 5), np.cumsum((Q_COLS,) + (G * DK,) * 6))]
    gt = w[:, Q_COLS + KV_COLS:NSA_COLS]

    def padded(z, width):
        n = z.shape[1] // DK
        return jnp.pad(z.reshape(D, n, DK), ((0, 0), (0, 0), (0, width - DK))).reshape(D, n * width)

    def padded_groups(z):
        n = z.shape[1] // G
        return jnp.pad(z.reshape(D, G, n), ((0, 0), (0, 0), (0, LANES - n))).reshape(D, G * LANES)

    token_major = jnp.concatenate(
        [padded(q, NSA_AUG), kc, vc, padded(vs, NSA_VAUG), padded(vw, NSA_VAUG),
         padded_groups(gt), w[:, NSA_COLS:NSA_COLS + 2 * SGU_WIDTH + RWKV_COLS]],
        axis=1).astype(BF16)
    keys_t = jnp.concatenate([padded(ks, NSA_AUG), padded(kw, NSA_AUG)], axis=1).T.astype(BF16)
    return token_major, keys_t


def _mix_proj(x, mod3, pre_g, w, wkt):
    B, S, D = x.shape
    tm = min(FFN_TILE, S)
    slopes = 2.0 ** -np.arange(1, NSA_HEADS + 1, dtype=np.float32)
    q_const = np.zeros((NSA_HEADS, NSA_AUG), np.float32)
    rest = (slopes.astype(np.float64) * np.log2(np.e))
    for piece in range(SLOPE_PIECES):
        part = rest.astype(jnp.bfloat16).astype(np.float64)
        q_const[:, NSA_DK + 2 * piece] = POS_SPLIT * part
        q_const[:, NSA_DK + 2 * piece + 1] = part
        rest = rest - part
    v_const = np.zeros((NSA_GROUPS, NSA_VAUG), np.float32)
    v_const[:, NSA_DK] = 1.0
    dtypes = (BF16, BF16, BF16, BF16, F32, F32, F32)
    tok = lambda n: pl.BlockSpec((None, tm, n), lambda b, i: (b, i, 0))
    return pl.pallas_call(
        _proj_kernel,
        grid=(B, S // tm),
        in_specs=[
            tok(D),
            pl.BlockSpec((None, 3, D), lambda b, i: (b, 0, 0)),
            _const_spec((1, D)),
            _const_spec((D, PROJ_COLS)),
            _const_spec((KT_ROWS, D)),
            _const_spec((1, Q_AUG_COLS)),
            _const_spec((1, V_AUG_COLS)),
        ],
        out_specs=[tok(n) for n in PROJ_WIDTHS] + [
            pl.BlockSpec((None, NSA_GROUPS, 1, NSA_AUG, tm), lambda b, i: (b, 0, i, 0, 0)),
            pl.BlockSpec((None, NSA_GROUPS, tm // LANES, NSA_AUG, LANES), lambda b, i: (b, 0, i, 0, 0))],
        out_shape=[jax.ShapeDtypeStruct((B, S, n), dt) for n, dt in zip(PROJ_WIDTHS, dtypes)] + [
            jax.ShapeDtypeStruct((B, NSA_GROUPS, S // tm, NSA_AUG, tm), BF16),
            jax.ShapeDtypeStruct((B, NSA_GROUPS, S // LANES, NSA_AUG, LANES), BF16)],
        compiler_params=_cparams("parallel", "parallel"),
        name="mix_proj",
    )(x, mod3, pre_g.reshape(1, D), w, wkt,
      jnp.asarray(q_const.reshape(1, -1)), jnp.asarray(v_const.reshape(1, -1)))


def _cmp_kernel(xk_ref, xv_ref, pek_ref, pev_ref, w1k_ref, w1v_ref, w2kt_ref, w2v_ref,
                kct_ref, vc_ref):
    half = xk_ref.shape[1]

    def hidden(x_ref, pe_ref, w1_ref):
        x = x_ref[...].astype(F32)
        n = x.shape[0]
        top = _dot(x + pe_ref[0:1, :], w1_ref[0:half, :])
        bot = _dot(x + pe_ref[1:2, :], w1_ref[half:2 * half, :])
        return _silu(top + pltpu.roll(bot, n - 1, axis=0))

    kct_ref[...] = _dot_nt(w2kt_ref[...], hidden(xk_ref, pek_ref, w1k_ref)).astype(BF16)
    vc_ref[...] = _dot(hidden(xv_ref, pev_ref, w1v_ref), w2v_ref[...]).astype(BF16)


def _compress(xk, xv, pe_k, pe_v, w1k, w1v, w2kt, w2v):
    B, NC, F = xk.shape
    G = NSA_GROUPS
    blk = lambda *s: pl.BlockSpec((None, None) + s, lambda b, g: (b, g, 0, 0))
    rows = pl.BlockSpec((None, NC, F), lambda b, g: (b, 0, 0))
    w1 = pl.BlockSpec((None, 2 * F, CMP_HID), lambda b, g: (g, 0, 0))
    return pl.pallas_call(
        _cmp_kernel,
        grid=(B, G),
        in_specs=[rows, rows, _const_spec((2, F)), _const_spec((2, F)), w1, w1,
                  _const_spec((NSA_DK, CMP_HID)), _const_spec((CMP_HID, NSA_VAUG))],
        out_specs=[blk(NSA_DK, NC), blk(NC, NSA_VAUG)],
        out_shape=[jax.ShapeDtypeStruct((B, G, NSA_DK, NC), BF16),
                   jax.ShapeDtypeStruct((B, G, NC, NSA_VAUG), BF16)],
        compiler_params=_cparams("parallel", "parallel"),
        name="nsa_compress",
    )(xk, xv, pe_k, pe_v, w1k, w1v, w2kt, w2v)


def _masked_softmax(s, valid):
    s = jnp.where(valid, s, NEG_INF)
    m = jnp.max(s, axis=-1, keepdims=True)
    p = jnp.where(valid, jnp.exp2(s - m), 0.0)
    l = jnp.sum(p, axis=-1, keepdims=True)
    return p * (1.0 / jnp.where(l > 0.0, l, 1.0))


def _nsa_kernel(q_ref, gt_ref, kc_ref, vc_ref, ks_ref, vs_ref, kw_ref, vw_ref, selt_ref, e_ref,
                o_ref, imp_ref, tiles_ref, *, n_top, tk):
    s0 = pl.program_id(2) * NSA_STEP
    R = NSA_HPG * Q_BLOCK
    n_slc = selt_ref.shape[0]
    n_cmp = kc_ref.shape[-1]
    parts = range(NSA_STEP // Q_BLOCK)
    each = lambda fn, *lists: [fn(*xs) for xs in zip(*lists)]
    rows = lambda c: slice(c * Q_BLOCK, (c + 1) * Q_BLOCK)
    head_cols = lambda h: slice(h * NSA_AUG, (h + 1) * NSA_AUG)
    q = [jnp.concatenate([q_ref[rows(c), head_cols(h)] for h in range(NSA_HPG)], axis=0) for c in parts]
    q_row = lax.broadcasted_iota(jnp.int32, (R, 1), 0) & (Q_BLOCK - 1)
    t_i = [s0 + c * Q_BLOCK + q_row for c in parts]

    def normalised(acc):
        return acc * (1.0 / acc[:, NSA_DK:NSA_DK + 1])

    cmp_end = lax.broadcasted_iota(jnp.int32, (1, n_cmp), 1) * CMP_STRIDE + (CMP_LEN - 1)
    s_c = [jnp.dot(x, kc_ref[...], preferred_element_type=F32) for x in q]
    p_c = each(lambda s, t: _masked_softmax(s, cmp_end <= t), s_c, t_i)
    o_c = [_dot(p, vc_ref[...]) for p in p_c]

    j = lax.broadcasted_iota(jnp.int32, (n_slc, Q_BLOCK), 0)
    imp = []
    for c in parts:
        p_sum = p_c[c][0:Q_BLOCK]
        for h in range(1, NSA_HPG):
            p_sum = p_sum + p_c[c][h * Q_BLOCK:(h + 1) * Q_BLOCK]
        hi, lo = _split(p_sum)
        raw = _dot_nt(selt_ref[...], hi) + _dot_nt(selt_ref[...], lo)
        tq = s0 + c * Q_BLOCK + lax.broadcasted_iota(jnp.int32, (n_slc, Q_BLOCK), 1)
        cur = tq // SLC_BLOCK
        forced = (j == 0) | (j == cur) | (j == cur - 1)
        live = j * SLC_BLOCK <= tq
        imp.append(jnp.where(forced, FORCE_SCORE, jnp.where(live, raw, NEG_INF)))
        imp_ref[c] = imp[c]

    def rank_body(i, cnt):
        tie = jnp.where(j > i, 1.0, 0.0)
        out = []
        for c in parts:
            r = imp_ref[c, pl.ds(i, 1), :]
            out.append(cnt[c] + jnp.where(r > imp[c], 1.0, jnp.where(r == imp[c], tie, 0.0)))
        return tuple(out)

    n_live = (s0 + NSA_STEP) // SLC_BLOCK
    cnt = lax.fori_loop(0, n_live, rank_body, tuple(jnp.zeros((n_slc, Q_BLOCK), F32) for _ in parts))
    lhs = []
    for c in parts:
        dropped = jnp.concatenate(
            [jnp.where(cnt[c] < n_top, 0.0, 1.0), jnp.zeros((NSA_AUG - n_slc, Q_BLOCK), F32)], axis=0)
        dropped = dropped.T.astype(BF16)
        lhs.append(jnp.concatenate([q[c], jnp.concatenate([dropped] * NSA_HPG, axis=0)], axis=1))

    last = (s0 + NSA_STEP - 1) // tk
    chosen = jnp.where(cnt[0] < n_top, 1.0, 0.0)
    for c in parts[1:]:
        chosen = jnp.maximum(chosen, jnp.where(cnt[c] < n_top, 1.0, 0.0))
    blocks_per_tile = tk // SLC_BLOCK
    n_used = jnp.int32(0)
    for kt in range(n_slc // blocks_per_tile):
        used = jnp.max(chosen[kt * blocks_per_tile:(kt + 1) * blocks_per_tile, :]) > 0.0
        tiles_ref[n_used] = kt
        n_used = n_used + jnp.where(used & (kt < last), 1, 0).astype(jnp.int32)

    def sel_tile(kt, carry, diagonal):
        m, acc = carry[0::2], carry[1::2]
        rhs = jnp.concatenate([ks_ref[kt], e_ref[kt]], axis=0)
        v = vs_ref[pl.ds(pl.multiple_of(kt * tk, tk), tk), :]
        s = [jnp.dot(x, rhs, preferred_element_type=F32) for x in lhs]
        if diagonal:
            pos = kt * tk + lax.broadcasted_iota(jnp.int32, (1, tk), 1)
            s = each(lambda x, t: jnp.where(pos <= t, x, NEG_INF), s, t_i)
        m_new = each(lambda x, y: jnp.maximum(x, jnp.max(y, axis=-1, keepdims=True)), m, s)
        alpha = each(lambda x, y: jnp.exp2(x - y), m, m_new)
        p = each(lambda x, y: jnp.exp2(x - y).astype(BF16), s, m_new)
        acc = each(lambda a, x, y: a * x + jnp.dot(y, v, preferred_element_type=F32), alpha, acc, p)
        out = []
        for x, y in zip(m_new, acc):
            out += [x, y]
        return tuple(out)

    init = []
    for _ in parts:
        init += [jnp.full((R, 1), NEG_INF, F32), jnp.zeros((R, NSA_VAUG), F32)]
    carry = lax.fori_loop(0, n_used, lambda i, x: sel_tile(tiles_ref[i], x, False), tuple(init))
    o_s = [normalised(x) for x in sel_tile(last, carry, True)[1::2]]

    n_wt = (WINDOW + Q_BLOCK) // LANES
    lane_pos = lax.broadcasted_iota(jnp.int32, (1, LANES), 1)
    s_w, start = [], []
    for c in parts:
        start.append(jnp.maximum(s0 + c * Q_BLOCK - WINDOW, 0))
        tiles = []
        for i in range(n_wt):
            pos = start[c] + i * LANES + lane_pos
            x = jnp.dot(q[c], kw_ref[start[c] // LANES + i], preferred_element_type=F32)
            x = jnp.where(pos <= t_i[c], x, NEG_INF)
            if i == 0:
                x = jnp.where(pos > t_i[c] - WINDOW, x, NEG_INF)
            tiles.append(x)
        s_w.append(jnp.concatenate(tiles, axis=1))
    p_w = [jnp.exp2(x - jnp.max(x, axis=-1, keepdims=True)) for x in s_w]
    o_w = [normalised(_dot(p, vw_ref[pl.ds(pl.multiple_of(st, LANES), n_wt * LANES), :]))
           for p, st in zip(p_w, start)]

    for c in parts:
        gate = _sigmoid(gt_ref[rows(c), :])
        for h in range(NSA_HPG):
            hr = slice(h * Q_BLOCK, (h + 1) * Q_BLOCK)
            o = (gate[:, 3 * h:3 * h + 1] * o_c[c][hr] + gate[:, 3 * h + 1:3 * h + 2] * o_s[c][hr]
                 + gate[:, 3 * h + 2:3 * h + 3] * o_w[c][hr])
            o_ref[rows(c), head_cols(h)] = o.astype(o_ref.dtype)


def _selection_map_t(n_cmp_rows, n_slc):
    rs, rc = SLC_BLOCK // CMP_STRIDE, CMP_LEN // CMP_STRIDE
    m = np.zeros((n_slc, n_cmp_rows), np.float32)
    for jj in range(n_slc):
        for a in range(rs):
            for b in range(rc):
                i = rs * jj - a - b
                if 0 <= i < n_cmp_rows - 1:
                    m[jj, i] += 1.0
    return jnp.asarray(m, BF16)


def _nsa_attention(q, gt, kc, vc, ks, vs, kw, vw):
    B, S, _ = q.shape
    G, A = NSA_GROUPS, NSA_AUG
    n_slc = S // SLC_BLOCK
    n_cmp = kc.shape[-1]
    n_kt, _, tk = ks.shape[2:]
    selt = _selection_map_t(n_cmp, n_slc)
    key_blk = np.arange(S).reshape(n_kt, 1, tk) // SLC_BLOCK
    e = np.where(np.arange(A).reshape(1, A, 1) == key_blk, -MASK_BIG, 0.0)
    heads = pl.BlockSpec((None, NSA_STEP, NSA_HPG * A), lambda b, g, i: (b, i, g))
    values = pl.BlockSpec((None, S, NSA_VAUG), lambda b, g, i: (b, 0, g))
    per_g = lambda *s: pl.BlockSpec((None, None) + s, lambda b, g, i: (b, g) + (0,) * len(s))
    kern = functools.partial(_nsa_kernel, n_top=min(SLC_TOPN, n_slc), tk=tk)
    return pl.pallas_call(
        kern,
        grid=(B, G, S // NSA_STEP),
        in_specs=[heads, pl.BlockSpec((None, NSA_STEP, LANES), lambda b, g, i: (b, i, g)),
                  per_g(A, n_cmp), per_g(n_cmp, NSA_VAUG),
                  per_g(n_kt, A, tk), values, per_g(S // LANES, A, LANES), values,
                  pl.BlockSpec((n_slc, n_cmp), lambda b, g, i: (0, 0)),
                  pl.BlockSpec((n_kt, A, tk), lambda b, g, i: (0, 0, 0))],
        out_specs=heads,
        out_shape=jax.ShapeDtypeStruct((B, S, NSA_HEADS * NSA_VAUG), BF16),
        scratch_shapes=[pltpu.VMEM((NSA_STEP // Q_BLOCK, n_slc, Q_BLOCK), F32),
                        pltpu.SMEM((n_kt,), jnp.int32)],
        compiler_params=_cparams("parallel", "parallel", "arbitrary"),
        name="nsa_attention",
    )(q, gt, kc, vc, ks, vs, kw, vw, selt, jnp.asarray(e, BF16))


def _with_position_rows(kt, pos):
    lead = kt.shape[:-2]
    n = kt.shape[-1]
    hi = jnp.broadcast_to(jnp.asarray(pos // POS_SPLIT, BF16)[..., None, :], lead + (1, n))
    lo = jnp.broadcast_to(jnp.asarray(pos % POS_SPLIT, BF16)[..., None, :], lead + (1, n))
    pad = jnp.zeros(lead + (NSA_AUG - NSA_DK - 2 * SLOPE_PIECES, n), BF16)
    return jnp.concatenate([kt] + [hi, lo] * SLOPE_PIECES + [pad], axis=-2)


def _nsa_mixer(q, kvc, vs, vw, gt, ks, kw, k_w1, k_w2, k_pe, v_w1, v_w2, v_pe):
    B, S, _ = q.shape
    G, DK = NSA_GROUPS, NSA_DK
    half = CMP_LEN // 2
    nc = S // half

    def half_blocks(z):
        return z.reshape(B, nc, half * G * DK)

    def both_groups(pe):
        return jnp.broadcast_to(pe[:, None, :], (CMP_LEN, G, DK)).reshape(2, half * G * DK)

    def per_group(w1):
        w1 = w1.reshape(CMP_LEN, DK, CMP_HID)
        return jnp.stack([jnp.zeros((CMP_LEN, G, DK, CMP_HID), F32).at[:, g].set(w1)
                          .reshape(CMP_LEN * G * DK, CMP_HID) for g in range(G)]).astype(BF16)

    kct, vcc = _compress(
        half_blocks(kvc[:, :, :G * DK]), half_blocks(kvc[:, :, G * DK:]),
        both_groups(k_pe), both_groups(v_pe),
        per_group(k_w1), per_group(v_w1), k_w2.T.astype(BF16),
        jnp.pad(v_w2, ((0, 0), (0, NSA_VAUG - DK))).astype(BF16))
    kca = _with_position_rows(kct, np.arange(nc) * CMP_STRIDE + (CMP_LEN - 1))
    return _nsa_attention(q, gt, kca, vcc, ks, vs, kw, vw)


def _gelu_tanh(x):
    return 0.5 * x * (1.0 + jnp.tanh(np.sqrt(2.0 / np.pi).astype(np.float32) * (x + 0.044715 * (x * x * x))))


def _sgu_kernel(p_ref, g_ref, b_ref, w_ref, bs_ref, o_ref):
    ts = p_ref.shape[0]
    gd = SGU_WIDTH // SGU_GROUPS
    u = _gelu_tanh(p_ref[:, 0:SGU_WIDTH])
    v = _gelu_tanh(p_ref[:, SGU_WIDTH:2 * SGU_WIDTH])
    mu = jnp.mean(v, axis=-1, keepdims=True)
    vc = v - mu
    var = jnp.mean(vc * vc, axis=-1, keepdims=True)
    vn = (vc * lax.rsqrt(var + LN_EPS) * g_ref[...] + b_ref[...]).astype(BF16)
    causal = (lax.broadcasted_iota(jnp.int32, (SGU_CHUNK, SGU_CHUNK), 0)
              >= lax.broadcasted_iota(jnp.int32, (SGU_CHUNK, SGU_CHUNK), 1))
    ws = [jnp.where(causal, w_ref[i], 0.0).astype(BF16) for i in range(SGU_GROUPS)]
    for n in range(ts // SGU_CHUNK):
        r0, r1 = n * SGU_CHUNK, (n + 1) * SGU_CHUNK
        s = jnp.concatenate(
            [jnp.dot(ws[i], vn[r0:r1, i * gd:(i + 1) * gd], preferred_element_type=F32)
             for i in range(SGU_GROUPS)], axis=1) + bs_ref[...]
        o_ref[r0:r1, :] = (u[r0:r1] * s).astype(o_ref.dtype)


def _sgu_mixer(p, ln_g, ln_b, w_s, b_s):
    B, S, _ = p.shape
    ts = min(SGU_TILE, S)
    gd = SGU_WIDTH // SGU_GROUPS
    bias = jnp.repeat(b_s.T, gd, axis=1)
    return pl.pallas_call(
        _sgu_kernel,
        grid=(B, S // ts),
        in_specs=[pl.BlockSpec((None, ts, 2 * SGU_WIDTH), lambda b, i: (b, i, 0)),
                  _const_spec((1, SGU_WIDTH)), _const_spec((1, SGU_WIDTH)),
                  _const_spec((SGU_GROUPS, SGU_CHUNK, SGU_CHUNK)), _const_spec((SGU_CHUNK, SGU_WIDTH))],
        out_specs=pl.BlockSpec((None, ts, SGU_WIDTH), lambda b, i: (b, i, 0)),
        out_shape=jax.ShapeDtypeStruct((B, S, SGU_WIDTH), BF16),
        compiler_params=_cparams("parallel", "parallel"),
        name="sgu",
    )(p, ln_g.reshape(1, -1), ln_b.reshape(1, -1), w_s, bias)


def _head_sum(x, bd_ref):
    return _dot_x2(x, bd_ref[...])


def _rwkv_operands(p_ref, carry, mu_ref, w0_ref, w2_ref, a0_ref, a2_ref, g2_ref, kk_ref, ka_ref, bd_ref,
                   v_res_refs):
    W = RWKV_WIDTH
    p = p_ref[...]
    tr = p.shape[0]

    @pl.when(pl.program_id(1) == 0)
    def _():
        carry[...] = jnp.zeros_like(carry)

    rowi = lax.broadcasted_iota(jnp.int32, (tr, 1), 0)
    p_prev = jnp.where(rowi == 0, carry[0:1, :], pltpu.roll(p, 1, axis=0))
    carry[0:1, :] = p[tr - 1:tr, :]
    ps = p + (p_prev - p) * mu_ref[...]
    r, k, v = ps[:, 0:W], ps[:, W:2 * W], ps[:, 2 * W:3 * W]
    wa = ps[:, 3 * W:3 * W + DECAY_LORA + AAA_LORA]
    gd = ps[:, 3 * W + DECAY_LORA + AAA_LORA:]
    w = w0_ref[...] + _dot(jnp.tanh(wa), w2_ref[...])
    x = -w
    softplus = jnp.maximum(x, 0.0) + jnp.log(1.0 + jnp.exp(-jnp.abs(x)))
    ld = -jnp.exp(-softplus - 0.5)
    a = _sigmoid(a0_ref[...] + _dot(wa, a2_ref[...]))
    g = _dot(_sigmoid(gd), g2_ref[...])
    if v_res_refs is not None:
        vf_ref, v0_ref, v1_ref, v2_ref = v_res_refs
        lora = _dot(_dot(v, v1_ref[...]), v2_ref[...])
        v = v + (vf_ref[...] - v) * _sigmoid(v0_ref[...] + lora)
    kk = k * kk_ref[...]
    norm = jnp.sqrt(_head_sum(kk * kk, bd_ref))
    kn = kk * (1.0 / jnp.maximum(norm, 1e-12))
    return r, ld, k * (1.0 + (a - 1.0) * ka_ref[...]), v, kn, kn * a, g


def _head_block_diag():
    idx = np.arange(RWKV_WIDTH) // RWKV_HEAD_DIM
    return jnp.asarray(idx[:, None] == idx[None, :], BF16)


def _rwkv_kernel(*refs, has_vres):
    (p_ref, mu_ref, w0_ref, w2_ref, a0_ref, a2_ref, g2_ref, kk_ref, ka_ref,
     rk_ref, lg_ref, lb_ref, bd_ref) = refs[:13]
    if has_vres:
        v_res_refs = refs[13:17]
        o_ref, state, carry = refs[17:]
    else:
        v_res_refs = None
        o_ref, v_ref, state, carry = refs[13:]
    r, ld, k, v, kn, ba, g = _rwkv_operands(p_ref, carry, mu_ref, w0_ref, w2_ref, a0_ref, a2_ref, g2_ref,
                                            kk_ref, ka_ref, bd_ref, v_res_refs)
    if not has_vres:
        v_ref[...] = v
    L = RWKV_CHUNK
    N = RWKV_HEAD_DIM
    NH = RWKV_PACK
    QW = NH * N
    n_sub = r.shape[0] // L

    @pl.when(pl.program_id(1) == 0)
    def _():
        state[...] = jnp.zeros_like(state)

    tt = lax.broadcasted_iota(jnp.int32, (n_sub * L, n_sub * L), 0)
    ts = lax.broadcasted_iota(jnp.int32, (n_sub * L, n_sub * L), 1)
    tril_ones = jnp.where((tt >= ts) & (tt // L == ts // L), 1.0, 0.0).astype(BF16)
    own = (lax.broadcasted_iota(jnp.int32, (QW, QW), 0) // N
           == lax.broadcasted_iota(jnp.int32, (QW, QW), 1) // N)
    ti = lax.broadcasted_iota(jnp.int32, (L, QW), 0)
    si = lax.broadcasted_iota(jnp.int32, (L, QW), 1) % L
    strict = ti > si
    incl = ti >= si
    eye_tok = jnp.where(ti == si, 1.0, 0.0)
    eye_ch = jnp.where(lax.broadcasted_iota(jnp.int32, (QW, QW), 0)
                       == lax.broadcasted_iota(jnp.int32, (QW, QW), 1), 1.0, 0.0)

    def stacked(x):
        x = x.astype(BF16)
        return jnp.where(own, jnp.concatenate([x] * NH, axis=0), jnp.zeros((), BF16))

    h1 = ld.astype(BF16)
    r1 = ld - h1.astype(F32)
    h2 = r1.astype(BF16)
    h3 = (r1 - h2.astype(F32)).astype(BF16)
    cum = _dot(tril_ones, h1) + (_dot(tril_ones, h2) + _dot(tril_ones, h3))
    w_in = jnp.exp(cum)
    w_inv = jnp.exp(-cum)
    rt = r * w_in
    kt = k * w_inv
    at = -kn * jnp.exp(cum - ld)
    bt = ba * w_inv

    n_grp = RWKV_WIDTH // QW
    chains = [(c, q) for c in range(n_sub) for q in range(n_grp)]
    each = lambda fn, *lists: [fn(*xs) for xs in zip(*lists)]
    pieces = lambda z: [z[c * L:(c + 1) * L, q * QW:(q + 1) * QW] for c, q in chains]
    cat0 = lambda *xs: jnp.concatenate(xs, axis=0)
    cat1 = lambda *xs: jnp.concatenate(xs, axis=1)
    a_t, r_t, b_t, k_t, v_t = pieces(at), pieces(rt), pieces(bt), pieces(kt), pieces(v)
    v_st = [stacked(x) for x in v_t]
    gram = each(lambda a, r, b, k: _dot_nt(cat0(a, r), cat0(stacked(b), stacked(k))), a_t, r_t, b_t, k_t)
    a_ab = [jnp.where(strict, x[0:L, 0:QW], 0.0) for x in gram]
    a_ak = [jnp.where(strict, x[0:L, QW:2 * QW], 0.0) for x in gram]
    m_rb = [jnp.where(incl, x[L:2 * L, 0:QW], 0.0) for x in gram]
    m_rk = [jnp.where(incl, x[L:2 * L, QW:2 * QW], 0.0) for x in gram]
    t_inv = [eye_tok + x for x in a_ab]
    power = each(lambda p: _dot(p, stacked(p)), a_ab)
    for _ in range(int(np.log2(L)) - 2):
        both = each(lambda p, t: _dot(cat0(p, t), stacked(p)), power, t_inv)
        power = [x[0:L] for x in both]
        t_inv = each(lambda t, x: t + x[L:2 * L], t_inv, both)
    t_inv = each(lambda t, p: t + _dot(t, stacked(p)), t_inv, power)
    akv = each(_dot, a_ak, v_st)
    au = each(lambda t, a, x: _dot(t, cat1(stacked(a), stacked(x))), t_inv, a_t, akv)
    ahat = [x[:, 0:QW] for x in au]
    uhat = [x[:, QW:2 * QW] for x in au]
    rhat = each(lambda r, m, a: r + _dot(m, stacked(a)), r_t, m_rb, ahat)
    y0 = each(lambda mb, mk, u, x: _dot(cat1(mb, mk), cat0(stacked(u), x)), m_rb, m_rk, uhat, v_st)
    w_last = [w_in[(c + 1) * L - 1:(c + 1) * L, q * QW:(q + 1) * QW] for c, q in chains]
    p_mat = each(lambda a, b, w: jnp.where(own, eye_ch + _dot_tn(a, b), 0.0) * w, ahat, b_t, w_last)
    q_mat = each(lambda u, x, b, k, w: jnp.where(own, _dot_tn(cat0(u, x), cat0(b, k)), 0.0) * w,
                 uhat, v_t, b_t, k_t, w_last)

    y_chunks = []
    for c in range(n_sub):
        idx = range(c * n_grp, (c + 1) * n_grp)
        s_old = [state[q] for q in range(n_grp)]
        y_chunks.append(jnp.concatenate([_dot_nt(rhat[i], s) + y0[i] for i, s in zip(idx, s_old)], axis=1))
        for q, i in enumerate(idx):
            state[q] = _dot(s_old[q], p_mat[i]) + q_mat[i]
    y = jnp.concatenate(y_chunks, axis=0)
    inv_n = 1.0 / N
    mean = _head_sum(y, bd_ref) * inv_n
    yc = y - mean
    var = _head_sum(yc * yc, bd_ref) * inv_n
    yn = yc * lax.rsqrt(var + LNX_EPS) * lg_ref[...] + lb_ref[...]
    bonus = _head_sum(r * k * rk_ref[...], bd_ref) * v
    o_ref[...] = ((yn + bonus) * g).astype(o_ref.dtype)


def _rwkv_mixer(p, mu, w0, w2, a0, a2, g2, k_k, k_a, r_k, lnx_g, lnx_b, v_first, v_res):
    B, S, C = p.shape
    W = RWKV_WIDTH
    step = min(RWKV_STEP, S)
    qw = RWKV_PACK * RWKV_HEAD_DIM
    row = lambda z: z.reshape(1, -1)
    lora_rows = DECAY_LORA + AAA_LORA
    w2p = jnp.zeros((lora_rows, W), F32).at[:DECAY_LORA].set(w2).astype(BF16)
    a2p = jnp.zeros((lora_rows, W), F32).at[DECAY_LORA:].set(a2).astype(BF16)
    tok = lambda n: pl.BlockSpec((None, step, n), lambda b, i: (b, i, 0))
    vec = _const_spec((1, W))
    args = [p, row(mu), row(w0), w2p, row(a0), a2p, g2.astype(BF16), row(k_k), row(k_a),
            row(r_k), row(lnx_g), row(lnx_b), _head_block_diag()]
    specs = [tok(C), _const_spec((1, C)), vec, _const_spec((lora_rows, W)), vec,
             _const_spec((lora_rows, W)), _const_spec((GATE_LORA, W)), vec, vec,
             vec, vec, vec, _const_spec((W, W))]
    out_specs = [tok(W)]
    out_shape = [jax.ShapeDtypeStruct((B, S, W), BF16)]
    if v_res is None:
        out_specs.append(tok(W))
        out_shape.append(jax.ShapeDtypeStruct((B, S, W), F32))
    else:
        v0, v1, v2 = v_res
        args += [v_first, row(v0), v1.astype(BF16), v2.astype(BF16)]
        specs += [tok(W), vec, _const_spec((W, MV_LORA)), _const_spec((MV_LORA, W))]
    out = pl.pallas_call(
        functools.partial(_rwkv_kernel, has_vres=v_res is not None),
        grid=(B, S // step),
        in_specs=specs,
        out_specs=out_specs,
        out_shape=out_shape,
        scratch_shapes=[pltpu.VMEM((W // qw, qw, qw), F32), pltpu.VMEM((8, C), F32)],
        compiler_params=_cparams("parallel", "arbitrary"),
        name="rwkv",
    )(*args)
    return out[0], (out[1] if v_res is None else v_first)


def _merge_kernel(x_ref, mod_ref, pg_ref, qg_ref, ya_ref, yb_ref, yc_ref, wg_ref,
                  wa_ref, wb_ref, wc_ref, ow_ref, o_ref):
    x = x_ref[...]
    D = x.shape[1]
    h = _pre(x, pg_ref[...], mod_ref).astype(BF16)
    merged = None
    for i, (y_ref, w_ref) in enumerate(((ya_ref, wa_ref), (yb_ref, wb_ref), (yc_ref, wc_ref))):
        gate = _sigmoid(jnp.dot(h, wg_ref[:, i * D:(i + 1) * D], preferred_element_type=F32))
        term = gate * jnp.dot(y_ref[...], w_ref[...], preferred_element_type=F32)
        merged = term if merged is None else merged + term
    y = _dot(merged, ow_ref[...])
    o_ref[...] = x + mod_ref[2:3, :] * _rms(y, qg_ref[...])


def _merge(x, mod3, pre_g, post_g, y_a, y_b, y_c, w_gate, branch_w, out_w):
    B, S, D = x.shape
    tm = min(FFN_TILE, S)
    tok = lambda n: pl.BlockSpec((None, tm, n), lambda b, i: (b, i, 0))
    w_a = jnp.pad(branch_w[0].reshape(NSA_HEADS, NSA_DK, D), ((0, 0), (0, NSA_VAUG - NSA_DK), (0, 0)))
    w_a = w_a.reshape(NSA_HEADS * NSA_VAUG, D)
    widths = (y_a.shape[-1], y_b.shape[-1], y_c.shape[-1])
    return pl.pallas_call(
        _merge_kernel,
        grid=(B, S // tm),
        in_specs=[tok(D), pl.BlockSpec((None, 3, D), lambda b, i: (b, 0, 0)),
                  _const_spec((1, D)), _const_spec((1, D))]
                 + [tok(n) for n in widths]
                 + [_const_spec((D, N_BRANCH * D))] + [_const_spec((n, D)) for n in widths]
                 + [_const_spec((D, D))],
        out_specs=tok(D),
        out_shape=jax.ShapeDtypeStruct((B, S, D), F32),
        compiler_params=_cparams("parallel", "parallel"),
        name="merge",
    )(x, mod3, pre_g.reshape(1, D), post_g.reshape(1, D), y_a, y_b, y_c, w_gate,
      w_a, branch_w[1], branch_w[2], out_w)


def kernel(x, c, ada_w, ada_b, pre_g, post_g, ffn_w_in, ffn_w_out, mix_w_in, branch_w, out_w,
           cmp_k_w1, cmp_k_w2, cmp_k_pe, cmp_v_w1, cmp_v_w2, cmp_v_pe,
           sgu_ln_g, sgu_ln_b, sgu_w, sgu_b,
           rwkv_mu, rwkv_w0, rwkv_w2, rwkv_a0, rwkv_a2, rwkv_g2, rwkv_kk, rwkv_ka, rwkv_rk,
           rwkv_lnx_g, rwkv_lnx_b, rwkv_v0, rwkv_v1, rwkv_v2):
    B, S, D = x.shape
    depth = ada_w.shape[0]
    mod = _ada_mod(c, ada_w, ada_b).reshape(depth, B, 3, 3, D)
    v_first = None
    for l in range(depth):
        x = _ffn(x, mod[l, :, 0], pre_g[l, 0], post_g[l, 0],
                 ffn_w_in[l, 0].astype(BF16), ffn_w_out[l, 0].astype(BF16))

        w = mix_w_in[l]
        n_in = NSA_COLS + 2 * SGU_WIDTH + RWKV_COLS
        q, kvc, vs, vw, gt, p_sgu, p_rwkv, ks, kw = _mix_proj(
            x, mod[l, :, 1], pre_g[l, 1], *_proj_weights(w))
        y_a = _nsa_mixer(q, kvc, vs, vw, gt, ks, kw, cmp_k_w1[l], cmp_k_w2[l], cmp_k_pe[l],
                         cmp_v_w1[l], cmp_v_w2[l], cmp_v_pe[l])
        y_b = _sgu_mixer(p_sgu, sgu_ln_g[l], sgu_ln_b[l], sgu_w[l], sgu_b[l])
        v_res = None if l == 0 else (rwkv_v0[l - 1], rwkv_v1[l - 1], rwkv_v2[l - 1])
        y_c, v_first = _rwkv_mixer(p_rwkv, rwkv_mu[l], rwkv_w0[l], rwkv_w2[l], rwkv_a0[l], rwkv_a2[l],
                                   rwkv_g2[l], rwkv_kk[l], rwkv_ka[l], rwkv_rk[l],
                                   rwkv_lnx_g[l], rwkv_lnx_b[l], v_first, v_res)
        x = _merge(x, mod[l, :, 1], pre_g[l, 1], post_g[l, 1], y_a, y_b, y_c,
                   w[:, n_in:].astype(BF16), branch_w[l].astype(BF16), out_w[l].astype(BF16))

        x = _ffn(x, mod[l, :, 2], pre_g[l, 2], post_g[l, 2],
                 ffn_w_in[l, 1].astype(BF16), ffn_w_out[l, 1].astype(BF16))
    return x
```
